```python
import jax, jax.numpy as jnp
from jax import lax
import numpy as np

D_MODEL = 4096
BATCH = 2
SEQ = 8192
DEPTH = 2
DEC_BATCH = 32
DEC_SEQ = 64
PAST_LEN = 2048

CHUNK = 64
N_BRANCH = 4
D_BRANCH = D_MODEL // N_BRANCH
CONV_A_WIDTH = 31
HEAD_B = 64
N_HEADS_B = D_BRANCH // HEAD_B
LORA_W = 64
LORA_A = 64
LORA_G = 128
P_B = 3 * D_BRANCH + LORA_W + LORA_A + LORA_G
GN_EPS_B = 64e-5
CONV_C_WIDTH = 4
N_BLOCKS_C = 16
BLOCK_C = D_BRANCH // N_BLOCKS_C
LRU_C = 8.0
MLP_CHUNK = 128
N_GROUPS_D = 16
GROUP_D = D_BRANCH // N_GROUPS_D
D_FF = 4 * D_MODEL
ALPHA = (2 * DEPTH) ** 0.25
BETA = (8 * DEPTH) ** -0.25
LN_EPS = 1e-5
OFF_A = 0
OFF_B = OFF_A + 2 * D_BRANCH
OFF_C = OFF_B + P_B
OFF_D = OFF_C + 2 * D_BRANCH
OFF_G = OFF_D + 2 * D_BRANCH
P_TOTAL = OFF_G + N_BRANCH * D_MODEL

kernel_name = "hybrid_stream_encoder_step"


def layer_norm(x, g, b, eps=LN_EPS):
    xf = x.astype(jnp.float32)
    mu = xf.mean(-1, keepdims=True)
    var = jnp.square(xf - mu).mean(-1, keepdims=True)
    return ((xf - mu) * lax.rsqrt(var + eps) * g + b).astype(x.dtype)


def causal_dwconv(x, hist, w, b):
    width = w.shape[0]
    xc = jnp.concatenate([hist.astype(x.dtype), x], axis=1)
    y = lax.conv_general_dilated(xc, w[:, None, :].astype(x.dtype), window_strides=(1,), padding='VALID',
                                 dimension_numbers=('NWC', 'WIO', 'NWC'), feature_group_count=x.shape[-1])
    return y + b, xc[:, -(width - 1):]


def mixer_a(p, hist, conv_w, conv_b, ln_g, ln_b):
    z = p[..., :D_BRANCH] * jax.nn.sigmoid(p[..., D_BRANCH:])
    y, new_hist = causal_dwconv(z, hist, conv_w, conv_b)
    return jax.nn.silu(layer_norm(y, ln_g, ln_b)), new_hist


def mixer_b(p, shift, wkv, mu, w0, w_w2, a0, w_a2, w_g2, k_k, k_a, r_k, gn_g, gn_b):
    bsz, t = p.shape[0], p.shape[1]
    p_prev = jnp.concatenate([shift[:, None].astype(p.dtype), p[:, :-1]], axis=1)
    ps = p + mu * (p_prev - p)
    r, k, v, w_lo, a_lo, g_lo = jnp.split(
        ps, [D_BRANCH, 2 * D_BRANCH, 3 * D_BRANCH, 3 * D_BRANCH + LORA_W, 3 * D_BRANCH + LORA_W + LORA_A], axis=-1)
    w = -jax.nn.softplus(-(w0 + jnp.tanh(w_lo) @ w_w2)) - 0.5
    decay = jnp.exp(-jnp.exp(w.astype(jnp.float32)))
    a = jax.nn.sigmoid(a0 + a_lo @ w_a2)
    g = jax.nn.sigmoid(g_lo) @ w_g2
    heads = lambda z: z.reshape(bsz, t, N_HEADS_B, HEAD_B).astype(jnp.float32)
    kk = heads(k * k_k)
    kk = kk / jnp.maximum(jnp.sqrt(jnp.sum(kk * kk, -1, keepdims=True)), 1e-12)
    k = heads(k * (1 + (a - 1) * k_a))
    r, v, a, dec = heads(r), heads(v), heads(a), heads(decay)

    def step(S, inp):
        r_t, dec_t, k_t, v_t, kk_t, a_t = inp
        S = (S * dec_t[:, :, None, :]
             - jnp.einsum('bhvk,bhk->bhv', S, kk_t)[..., None] * (kk_t * a_t)[:, :, None, :]
             + v_t[..., None] * k_t[:, :, None, :])
        return S, jnp.einsum('bhvk,bhk->bhv', S, r_t)

    xs = tuple(jnp.moveaxis(z, 1, 0) for z in (r, dec, k, v, kk, a))
    S, y = lax.scan(step, wkv.astype(jnp.float32), xs)
    y = jnp.moveaxis(y, 0, 1)
    mu_y = y.mean(-1, keepdims=True)
    var_y = jnp.square(y - mu_y).mean(-1, keepdims=True)
    y = ((y - mu_y) * lax.rsqrt(var_y + GN_EPS_B)).reshape(bsz, t, D_BRANCH) * gn_g + gn_b
    bonus = (jnp.sum(r * k * r_k, -1, keepdims=True) * v).reshape(bsz, t, D_BRANCH)
    return ((y + bonus) * g).astype(p.dtype), p[:, -1], S


def mixer_c(p, hist, h0, conv_w, conv_b, wa, ba, wx, bx, lam):
    xb, gate = p[..., :D_BRANCH], p[..., D_BRANCH:]
    xb, new_hist = causal_dwconv(xb, hist, conv_w, conv_b)
    bsz, t = xb.shape[0], xb.shape[1]
    xblk = xb.reshape(bsz, t, N_BLOCKS_C, BLOCK_C)
    r = jax.nn.sigmoid(jnp.einsum('btni,nij->btnj', xblk, wa).reshape(bsz, t, D_BRANCH) + ba)
    i = jax.nn.sigmoid(jnp.einsum('btni,nij->btnj', xblk, wx).reshape(bsz, t, D_BRANCH) + bx)
    log_a = (LRU_C * r * jax.nn.log_sigmoid(lam)).astype(jnp.float32)
    a = jnp.exp(log_a)
    u = jnp.sqrt(-jnp.expm1(2 * log_a)) * (i * xb).astype(jnp.float32)

    def combine(e1, e2):
        return e1[0] * e2[0], e2[0] * e1[1] + e2[1]

    a_cum, h = lax.associative_scan(combine, (a, u), axis=1)
    h = h + a_cum * h0[:, None].astype(jnp.float32)
    return h.astype(p.dtype) * jax.nn.gelu(gate), new_hist, h[:, -1]


def mixer_d(p, ln_g, ln_b, w_s, b_s):
    z = jax.nn.gelu(p)
    u, v = z[..., :D_BRANCH], layer_norm(z[..., D_BRANCH:], ln_g, ln_b)
    bsz, t = v.shape[0], v.shape[1]
    n_chunks = -(-t // MLP_CHUNK)
    pad = n_chunks * MLP_CHUNK - t
    vc = jnp.pad(v, ((0, 0), (0, pad), (0, 0))).reshape(bsz, n_chunks, MLP_CHUNK, N_GROUPS_D, GROUP_D)
    blk = jnp.arange(MLP_CHUNK) // CHUNK
    ws = jnp.where(blk[:, None] >= blk[None, :], w_s, 0)
    s = jnp.einsum('gij,bcjgd->bcigd', ws.astype(v.dtype), vc) + b_s.T[:, :, None]
    s = s.reshape(bsz, n_chunks * MLP_CHUNK, D_BRANCH)[:, :t]
    return u * s, v


def setup_inputs(seed: int = 0) -> dict:
    key = jax.random.key(seed)
    ks = list(jax.random.split(key, 48))

    def nrm(shape, scale):
        return scale * jax.random.normal(ks.pop(), shape, jnp.float32)

    def uni(shape, lo, hi):
        return jax.random.uniform(ks.pop(), shape, jnp.float32, lo, hi)

    L = DEPTH
    u = uni((L, D_BRANCH), 0.9, 0.999)
    s = u ** (1.0 / LRU_C)
    return {
        'x_prompt': nrm((BATCH, SEQ, D_MODEL), 1.0),
        'x_sample': nrm((DEC_BATCH, DEC_SEQ, D_MODEL), 1.0),
        'state_conv_a': nrm((L, DEC_BATCH, CONV_A_WIDTH - 1, D_BRANCH), 0.5),
        'state_shift_b': nrm((L, DEC_BATCH, P_B), 1.0),
        'state_wkv_b': nrm((L, DEC_BATCH, N_HEADS_B, HEAD_B, HEAD_B), 0.3),
        'state_conv_c': nrm((L, DEC_BATCH, CONV_C_WIDTH - 1, D_BRANCH), 1.0),
        'state_lru_c': nrm((L, DEC_BATCH, D_BRANCH), 0.5),
        'ln_in_g': 1.0 + nrm((D_MODEL,), 0.02),
        'ln_in_b': nrm((D_MODEL,), 0.02),
        'w_in': nrm((L, D_MODEL, P_TOTAL), D_MODEL ** -0.5),
        'conv_a_w': nrm((L, CONV_A_WIDTH, D_BRANCH), CONV_A_WIDTH ** -0.5),
        'conv_a_b': nrm((L, D_BRANCH), 0.02),
        'ln_a_g': 1.0 + nrm((L, D_BRANCH), 0.02),
        'ln_a_b': nrm((L, D_BRANCH), 0.02),
        'mu_b': uni((L, P_B), 0.0, 1.0),
        'w0_b': uni((L, D_BRANCH), -6.0, 0.0),
        'w_w2_b': nrm((L, LORA_W, D_BRANCH), 0.1),
        'a0_b': nrm((L, D_BRANCH), 0.1),
        'w_a2_b': nrm((L, LORA_A, D_BRANCH), 0.1),
        'w_g2_b': nrm((L, LORA_G, D_BRANCH), LORA_G ** -0.5),
        'k_k_b': 0.85 + nrm((L, D_BRANCH), 0.02),
        'k_a_b': 1.0 + nrm((L, D_BRANCH), 0.02),
        'r_k_b': nrm((L, N_HEADS_B, HEAD_B), 0.1),
        'gn_b_g': 1.0 + nrm((L, D_BRANCH), 0.02),
        'gn_b_b': nrm((L, D_BRANCH), 0.02),
        'conv_c_w': nrm((L, CONV_C_WIDTH, D_BRANCH), 0.5),
        'conv_c_b': nrm((L, D_BRANCH), 0.02),
        'lru_wa': nrm((L, N_BLOCKS_C, BLOCK_C, BLOCK_C), BLOCK_C ** -0.5),
        'lru_ba': nrm((L, D_BRANCH), 0.02),
        'lru_wx': nrm((L, N_BLOCKS_C, BLOCK_C, BLOCK_C), BLOCK_C ** -0.5),
        'lru_bx': nrm((L, D_BRANCH), 0.02),
        'lru_lambda': jnp.log(s) - jnp.log1p(-s),
        'ln_d_g': 1.0 + nrm((L, D_BRANCH), 0.02),
        'ln_d_b': nrm((L, D_BRANCH), 0.02),
        'w_s_d': nrm((L, N_GROUPS_D, MLP_CHUNK, MLP_CHUNK), MLP_CHUNK ** -0.5),
        'b_s_d': 1.0 + nrm((L, N_GROUPS_D, MLP_CHUNK), 0.02),
        'w_branch': nrm((L, N_BRANCH, D_BRANCH, D_MODEL), D_BRANCH ** -0.5),
        'w_out': nrm((L, D_MODEL, D_MODEL), BETA * D_MODEL ** -0.5),
        'ln1_g': 1.0 + nrm((L, D_MODEL), 0.02),
        'ln1_b': nrm((L, D_MODEL), 0.02),
        'w_up': nrm((L, D_MODEL, D_FF), D_MODEL ** -0.5),
        'w_down': nrm((L, D_FF, D_MODEL), BETA * D_FF ** -0.5),
        'ln2_g': 1.0 + nrm((L, D_MODEL), 0.02),
        'ln2_b': nrm((L, D_MODEL), 0.02),
    }


def reference(x_prompt, x_sample, state_conv_a, state_shift_b, state_wkv_b, state_conv_c, state_lru_c,
              ln_in_g, ln_in_b, w_in, conv_a_w, conv_a_b, ln_a_g, ln_a_b,
              mu_b, w0_b, w_w2_b, a0_b, w_a2_b, w_g2_b, k_k_b, k_a_b, r_k_b, gn_b_g, gn_b_b,
              conv_c_w, conv_c_b, lru_wa, lru_ba, lru_wx, lru_bx, lru_lambda,
              ln_d_g, ln_d_b, w_s_d, b_s_d,
              w_branch, w_out, ln1_g, ln1_b, w_up, w_down, ln2_g, ln2_b):

    def run(x, st_a, st_bs, st_bw, st_cc, st_ch):
        x = layer_norm(x, ln_in_g, ln_in_b)
        new_a, new_bs, new_bw, new_cc, new_ch, v_rows = [], [], [], [], [], []
        for l in range(DEPTH):
            proj = x @ w_in[l]
            y_a, n_a = mixer_a(proj[..., OFF_A:OFF_B], st_a[l], conv_a_w[l], conv_a_b[l], ln_a_g[l], ln_a_b[l])
            y_b, n_bs, n_bw = mixer_b(proj[..., OFF_B:OFF_C], st_bs[l], st_bw[l], mu_b[l], w0_b[l], w_w2_b[l],
                                      a0_b[l], w_a2_b[l], w_g2_b[l], k_k_b[l], k_a_b[l], r_k_b[l],
                                      gn_b_g[l], gn_b_b[l])
            y_c, n_cc, n_ch = mixer_c(proj[..., OFF_C:OFF_D], st_cc[l], st_ch[l], conv_c_w[l], conv_c_b[l],
                                      lru_wa[l], lru_ba[l], lru_wx[l], lru_bx[l], lru_lambda[l])
            y_d, v_d = mixer_d(proj[..., OFF_D:OFF_G], ln_d_g[l], ln_d_b[l], w_s_d[l], b_s_d[l])
            merged = None
            for n, y_n in enumerate((y_a, y_b, y_c, y_d)):
                gate = jax.nn.sigmoid(proj[..., OFF_G + n * D_MODEL:OFF_G + (n + 1) * D_MODEL])
                term = gate * (y_n @ w_branch[l, n])
                merged = term if merged is None else merged + term
            x = layer_norm(ALPHA * x + merged @ w_out[l], ln1_g[l], ln1_b[l])
            hdn = jnp.square(jax.nn.relu(x @ w_up[l]))
            x = layer_norm(ALPHA * x + hdn @ w_down[l], ln2_g[l], ln2_b[l])
            new_a.append(n_a); new_bs.append(n_bs); new_bw.append(n_bw)
            new_cc.append(n_cc); new_ch.append(n_ch); v_rows.append(v_d)
        return (x, jnp.stack(new_a), jnp.stack(new_bs), jnp.stack(new_bw),
                jnp.stack(new_cc), jnp.stack(new_ch), v_rows)

    dt = x_prompt.dtype
    y_prompt, p_conv_a, p_shift_b, p_wkv_b, p_conv_c, p_lru_c, _ = run(
        x_prompt,
        jnp.zeros((DEPTH, BATCH, CONV_A_WIDTH - 1, D_BRANCH), dt),
        jnp.zeros((DEPTH, BATCH, P_B), dt),
        jnp.zeros((DEPTH, BATCH, N_HEADS_B, HEAD_B, HEAD_B), jnp.float32),
        jnp.zeros((DEPTH, BATCH, CONV_C_WIDTH - 1, D_BRANCH), dt),
        jnp.zeros((DEPTH, BATCH, D_BRANCH), dt))
    y_sample, s_conv_a, s_shift_b, s_wkv_b, s_conv_c, s_lru_c, v_rows_s = run(
        x_sample, state_conv_a, state_shift_b, state_wkv_b, state_conv_c, state_lru_c)
    s_rows_d = jnp.stack(v_rows_s)
    return (y_prompt, y_sample, p_conv_a, p_shift_b, p_wkv_b, p_conv_c, p_lru_c,
            s_conv_a, s_shift_b, s_wkv_b, s_conv_c, s_lru_c, s_rows_d)
```

```python
import functools
import math

import jax
import jax.numpy as jnp
from jax import lax
from jax.experimental import pallas as pl
from jax.experimental.pallas import tpu as pltpu

F32 = jnp.float32
BF16 = jnp.bfloat16

LANES = 128
HEAD = 64
WKV_CHUNK = 64
MLP_CHUNK = 128
STREAM_CHUNK = 64
VMEM_LIMIT = 56 * 1024 * 1024
LN_EPS = 1e-5
GN_EPS_B = 64e-5
LRU_C = 8.0


def _cparams(n_axes):
    return pltpu.CompilerParams(dimension_semantics=("arbitrary",) * n_axes, vmem_limit_bytes=VMEM_LIMIT)


def _dot(a, b, dims=(((1,), (0,)), ((), ()))):
    return lax.dot_general(a.astype(BF16), b.astype(BF16), dims, preferred_element_type=F32)


_NT = (((1,), (1,)), ((), ()))


def _hi_lo(x):
    hi = x.astype(BF16)
    lo = (x - hi.astype(F32)).astype(BF16)
    return hi, lo


def _lhs3(x):
    hi, lo = _hi_lo(x)
    return jnp.concatenate([hi, hi, lo], axis=1)


def _rhs3(w):
    hi, lo = _hi_lo(w.astype(F32))
    return jnp.concatenate([hi, lo, hi], axis=0)


def _gelu(x):
    return 0.5 * x * (1.0 + jnp.tanh(math.sqrt(2.0 / math.pi) * (x + 0.044715 * (x * x * x))))


def _sigmoid(x):
    return 1.0 / (1.0 + jnp.exp(-x))


def _softplus(x):
    return jnp.maximum(x, 0.0) + jnp.log1p(jnp.exp(-jnp.abs(x)))


def _layer_norm(x, g, b, eps=LN_EPS):
    mu = jnp.mean(x, axis=-1, keepdims=True)
    xc = x - mu
    var = jnp.mean(xc * xc, axis=-1, keepdims=True)
    return xc * lax.rsqrt(var + eps) * g + b


def _iota(shape, dim):
    return lax.broadcasted_iota(jnp.int32, shape, dim)


def _mm_kernel(a_ref, w_ref, o_ref, *, act):
    acc = jnp.dot(a_ref[...], w_ref[...], preferred_element_type=F32)
    if act == "relu2":
        acc = jnp.square(jnp.maximum(acc, 0.0))
    o_ref[...] = acc.astype(o_ref.dtype)


def _tile(extent, target, quantum=LANES):
    best = None
    for cand in range(quantum, min(extent, target) + 1, quantum):
        if extent % cand == 0:
            best = cand
    assert best is not None, (extent, target, quantum)
    return best


def _mm(a, w, out_dtype, tm, tn, act=None):
    m, k = a.shape
    n = w.shape[1]
    tm, tn = _tile(m, tm), _tile(n, tn)
    return pl.pallas_call(
        functools.partial(_mm_kernel, act=act),
        grid=(m // tm, n // tn),
        in_specs=[pl.BlockSpec((tm, k), lambda i, j: (i, 0)),
                  pl.BlockSpec((k, tn), lambda i, j: (0, j))],
        out_specs=pl.BlockSpec((tm, tn), lambda i, j: (i, j)),
        out_shape=jax.ShapeDtypeStruct((m, n), out_dtype),
        compiler_params=_cparams(2),
    )(a, w)


def _mm_acc_kernel(a_ref, w_ref, o_ref, acc_ref):
    kk = pl.program_id(2)

    @pl.when(kk == 0)
    def _():
        acc_ref[...] = jnp.zeros_like(acc_ref)

    acc_ref[...] += jnp.dot(a_ref[...], w_ref[...], preferred_element_type=F32)

    @pl.when(kk == pl.num_programs(2) - 1)
    def _():
        o_ref[...] = acc_ref[...].astype(o_ref.dtype)


def _mm_acc(a, w, out_dtype, tm, tn, tk):
    m, k = a.shape
    n = w.shape[1]
    tm, tn, tk = _tile(m, tm), _tile(n, tn), _tile(k, tk)
    return pl.pallas_call(
        _mm_acc_kernel,
        grid=(m // tm, n // tn, k // tk),
        in_specs=[pl.BlockSpec((tm, tk), lambda i, j, q: (i, q)),
                  pl.BlockSpec((tk, tn), lambda i, j, q: (q, j))],
        out_specs=pl.BlockSpec((tm, tn), lambda i, j, q: (i, j)),
        out_shape=jax.ShapeDtypeStruct((m, n), out_dtype),
        scratch_shapes=[pltpu.VMEM((tm, tn), F32)],
        compiler_params=_cparams(3),
    )(a, w)


def _ln_kernel(x_ref, g_ref, b_ref, o_ref, ob_ref):
    y = _layer_norm(x_ref[...], g_ref[...], b_ref[...])
    o_ref[...] = y
    ob_ref[...] = y.astype(BF16)


def _ln_res_kernel(x_ref, r_ref, g_ref, b_ref, o_ref, ob_ref, *, alpha):
    y = _layer_norm(alpha * x_ref[...] + r_ref[...], g_ref[...], b_ref[...])
    o_ref[...] = y
    ob_ref[...] = y.astype(BF16)


def _ln_call(x, res, g, b, alpha, tm):
    m, d = x.shape
    tm = _tile(m, tm, 8)
    row = pl.BlockSpec((tm, d), lambda i: (i, 0))
    vec = pl.BlockSpec((1, d), lambda i: (0, 0))
    if res is None:
        kern, args, specs = _ln_kernel, (x, g.reshape(1, d), b.reshape(1, d)), [row, vec, vec]
    else:
        kern = functools.partial(_ln_res_kernel, alpha=alpha)
        args, specs = (x, res, g.reshape(1, d), b.reshape(1, d)), [row, row, vec, vec]
    return pl.pallas_call(
        kern, grid=(m // tm,), in_specs=specs, out_specs=[row, row],
        out_shape=[jax.ShapeDtypeStruct((m, d), F32), jax.ShapeDtypeStruct((m, d), BF16)],
        compiler_params=_cparams(1),
    )(*args)


def _merge_kernel(ya_ref, yb_ref, yc_ref, yd_ref, g0_ref, g1_ref, g2_ref, g3_ref, wb_ref, o_ref):
    acc = None
    for n, (y_ref, g_ref) in enumerate(((ya_ref, g0_ref), (yb_ref, g1_ref), (yc_ref, g2_ref), (yd_ref, g3_ref))):
        br = jnp.dot(y_ref[...], wb_ref[n], preferred_element_type=F32)
        term = _sigmoid(g_ref[...].astype(F32)) * br
        acc = term if acc is None else acc + term
    o_ref[...] = acc.astype(o_ref.dtype)


def _merge(ys, gates, wb, tm, tn):
    m, db = ys[0].shape
    d = wb.shape[2]
    tm, tn = _tile(m, tm), _tile(d, tn)
    nj = d // tn
    y_spec = pl.BlockSpec((tm, db), lambda j, i: (i, 0))
    g_specs = [pl.BlockSpec((tm, tn), functools.partial(lambda j, i, n: (i, n * nj + j), n=n)) for n in range(4)]
    return pl.pallas_call(
        _merge_kernel,
        grid=(nj, m // tm),
        in_specs=[y_spec] * 4 + g_specs + [pl.BlockSpec((4, db, tn), lambda j, i: (0, 0, j))],
        out_specs=pl.BlockSpec((tm, tn), lambda j, i: (i, j)),
        out_shape=jax.ShapeDtypeStruct((m, d), BF16),
        compiler_params=_cparams(2),
    )(*ys, gates, gates, gates, gates, wb)


def _mixer_a_kernel(a1_ref, a2_ref, hist_ref, cw_ref, cb_ref, g_ref, b_ref, y_ref, nh_ref, zext_ref, acc_ref,
                    *, tb, width, pad):
    hist = width - 1
    nchunk = zext_ref.shape[0]
    t = pl.program_id(1)

    @pl.when(t == 0)
    def _():
        h = hist_ref[0]
        for c in range(nchunk):
            zext_ref[c, 0:pad - hist, :] = jnp.zeros((pad - hist, LANES), F32)
            zext_ref[c, pad - hist:pad, :] = h[:, c * LANES:(c + 1) * LANES]

    z = a1_ref[...] * _sigmoid(a2_ref[...])
    for c in range(nchunk):
        zext_ref[c, pad:pad + tb, :] = z[:, c * LANES:(c + 1) * LANES]

    rc = 64

    def conv_chunk(c, carry):
        w = cw_ref[c]
        bias = cb_ref[c]
        for r0 in range(0, tb, rc):
            acc = jnp.broadcast_to(bias, (rc, LANES))
            for j in range(width):
                acc = acc + w[j:j + 1, :] * zext_ref[c, pl.ds(pad - hist + j + r0, rc), :]
            acc_ref[c, r0:r0 + rc, :] = acc
        return carry

    lax.fori_loop(0, nchunk, conv_chunk, 0)
    y = jnp.concatenate([acc_ref[c] for c in range(nchunk)], axis=1)
    y = _layer_norm(y, g_ref[...], b_ref[...])
    y_ref[...] = (y * _sigmoid(y)).astype(y_ref.dtype)
    nh_ref[0] = jnp.concatenate([zext_ref[c, tb + pad - hist:tb + pad, :] for c in range(nchunk)], axis=1)
    for c in range(nchunk):
        zext_ref[c, 0:pad, :] = zext_ref[c, tb:tb + pad, :]


def _mixer_c_kernel(xb_ref, gate_ref, hist_ref, h0_ref, cw_ref, cb_ref, wa_ref, ba_ref, wx_ref, bx_ref, lam_ref,
                    y_ref, nhist_ref, nh_ref, xext_ref, a_ref, u_ref, hs_ref, hcar_ref, *, tb, width, pad):
    hist = width - 1
    t = pl.program_id(1)
    db = xb_ref.shape[1]

    @pl.when(t == 0)
    def _():
        xext_ref[0:pad - hist, :] = jnp.zeros((pad - hist, db), F32)
        xext_ref[pad - hist:pad, :] = hist_ref[0]
        hcar_ref[...] = h0_ref[0]

    xext_ref[pad:pad + tb, :] = xb_ref[...]
    cw = cw_ref[...]
    xc = jnp.broadcast_to(cb_ref[...], (tb, db))
    for j in range(width):
        xc = xc + cw[j:j + 1, :] * xext_ref[pad - hist + j:pad - hist + j + tb, :]
    xcb = xc.astype(BF16)
    r = _sigmoid(jnp.dot(xcb, wa_ref[...], preferred_element_type=F32) + ba_ref[...])
    i = _sigmoid(jnp.dot(xcb, wx_ref[...], preferred_element_type=F32) + bx_ref[...])
    log_a = (LRU_C * r) * (-_softplus(-lam_ref[...]))
    a_ref[...] = jnp.exp(log_a)
    u_ref[...] = jnp.sqrt(1.0 - jnp.exp(2.0 * log_a)) * (i * xc)

    def step(k, h):
        h = a_ref[pl.ds(k, 1), :] * h + u_ref[pl.ds(k, 1), :]
        hs_ref[pl.ds(k, 1), :] = h
        return h

    h = lax.fori_loop(0, tb, step, hcar_ref[...], unroll=8)
    hcar_ref[...] = h
    y_ref[...] = (hs_ref[...] * _gelu(gate_ref[...])).astype(y_ref.dtype)
    nhist_ref[0] = xext_ref[tb + pad - hist:tb + pad, :]
    nh_ref[0] = h
    xext_ref[0:pad, :] = xext_ref[tb:tb + pad, :]


def _mixer_d_kernel(u_ref, v_ref, g_ref, b_ref, ws_ref, bias_ref, y_ref, *maybe_vrows_ref, tb, rows):
    v = _layer_norm(_gelu(v_ref[...]), g_ref[...], b_ref[...])
    for vrows_ref in maybe_vrows_ref:
        vrows_ref[...] = v
    vb = v.astype(BF16)
    npair = v.shape[1] // LANES
    first = _iota((rows, LANES), 1) < HEAD
    zero = jnp.zeros((rows, LANES), BF16)
    for r0 in range(0, tb, rows):
        for p in range(npair):
            ls = slice(p * LANES, (p + 1) * LANES)
            v2 = vb[r0:r0 + rows, ls]
            vbd = jnp.concatenate([jnp.where(first, v2, zero), jnp.where(first, zero, v2)], axis=0)
            s = jnp.dot(ws_ref[p], vbd, preferred_element_type=F32) + bias_ref[:, ls]
            y_ref[r0:r0 + rows, ls] = (_gelu(u_ref[r0:r0 + rows, ls]) * s).astype(y_ref.dtype)


def _segsum(x, ones2):
    outs = []
    for c in range(x.shape[1] // LANES):
        hi, lo = _hi_lo(x[:, c * LANES:(c + 1) * LANES])
        outs.append(jnp.dot(jnp.concatenate([hi, lo], axis=1), ones2, preferred_element_type=F32))
    return jnp.concatenate(outs, axis=1)


def _wkv_pair(at2, rt2, bt2, kt2, v2, s0, pc2):
    c = WKV_CHUNK
    lane = _iota((c, LANES), 1)
    first = lane < HEAD

    def bd(x):
        return jnp.concatenate([jnp.where(first, x, 0.0), jnp.where(first, 0.0, x)], axis=0)

    g = _dot(jnp.concatenate([at2, rt2], axis=0), jnp.concatenate([bd(bt2), bd(kt2)], axis=0), _NT)
    row = _iota((c, LANES), 0)
    col = jnp.where(first, lane, lane - HEAD)
    strict = row > col
    incl = row >= col
    a_ab = jnp.where(strict, g[:c, :LANES], 0.0)
    a_ak = jnp.where(strict, g[:c, LANES:], 0.0)
    a_rb = jnp.where(incl, g[c:, :LANES], 0.0)
    a_rk = jnp.where(incl, g[c:, LANES:], 0.0)
    a_pow = bd(a_ab)
    eye = (_iota((LANES, LANES), 0) == _iota((LANES, LANES), 1)).astype(F32)
    x = eye + a_pow
    for _ in range(5):
        a_pow = _dot(a_pow, a_pow)
        x = x + _dot(x, a_pow)
    t_inv = x[:c] + x[c:]
    s0b = s0.astype(BF16)
    vbd = bd(v2)
    rhs = _dot(at2, s0b, _NT) + _dot(a_ak, vbd)
    u2 = _dot(t_inv, bd(rhs))
    y2 = _dot(rt2, s0b, _NT) + _dot(jnp.concatenate([a_rb, a_rk], axis=1), jnp.concatenate([bd(u2), vbd], axis=0))
    uvt = jnp.transpose(jnp.concatenate([u2, v2], axis=0))
    upd = _dot(uvt, jnp.concatenate([bt2, kt2], axis=0))
    same_head = (_iota((LANES, LANES), 0) < HEAD) == (_iota((LANES, LANES), 1) < HEAD)
    s1 = jnp.where(same_head, (s0 + upd) * pc2, 0.0)
    return y2, s1


def _mixer_b_kernel(pr_ref, pk_ref, pv_ref, pl_ref, sh_ref, shl_ref, s0_ref,
                    mu_ref, mul_ref, w0_ref, ww2_ref, a0_ref, wa2_ref, wg2_ref, kk_ref, ka_ref, rk_ref, gg_ref, gb_ref,
                    y_ref, nsh_ref, nshl_ref, ns_ref,
                    pext_ref, lext_ref, st_ref, r_s, k_s, v_s, kk_s, b_s, ld_s, y_s, *, tb, pad):
    t = pl.program_id(1)
    db = pr_ref.shape[1]
    npair = db // LANES

    @pl.when(t == 0)
    def _():
        pext_ref[0:pad, :] = jnp.broadcast_to(sh_ref[0], (pad, 3 * db))
        lext_ref[0:pad, :] = jnp.broadcast_to(shl_ref[0], (pad, lext_ref.shape[1]))
        st_ref[...] = s0_ref[0]

    pext_ref[pad:pad + tb, 0:db] = pr_ref[...]
    pext_ref[pad:pad + tb, db:2 * db] = pk_ref[...]
    pext_ref[pad:pad + tb, 2 * db:3 * db] = pv_ref[...]
    lext_ref[pad:pad + tb, :] = pl_ref[...]
    p = pext_ref[pad:pad + tb, :]
    ps = p + mu_ref[...] * (pext_ref[pad - 1:pad - 1 + tb, :] - p)
    lo_ = lext_ref[pad:pad + tb, :]
    lo_s = lo_ + mul_ref[...] * (lext_ref[pad - 1:pad - 1 + tb, :] - lo_)
    nsh_ref[0] = pext_ref[pad + tb - 1:pad + tb, :]
    nshl_ref[0] = lext_ref[pad + tb - 1:pad + tb, :]
    pext_ref[0:pad, :] = pext_ref[tb:tb + pad, :]
    lext_ref[0:pad, :] = lext_ref[tb:tb + pad, :]

    r = ps[:, 0:db]
    k = ps[:, db:2 * db]
    v = ps[:, 2 * db:3 * db]
    wa_lo = lo_s[:, 0:LANES]
    g_lo = lo_s[:, LANES:2 * LANES]
    w = -_softplus(-(w0_ref[...] + jnp.dot(_lhs3(jnp.tanh(wa_lo)), ww2_ref[...], preferred_element_type=F32))) - 0.5
    a = _sigmoid(a0_ref[...] + jnp.dot(_lhs3(wa_lo), wa2_ref[...], preferred_element_type=F32))
    gate = jnp.dot(_lhs3(_sigmoid(g_lo)), wg2_ref[...], preferred_element_type=F32)

    lane2 = _iota((2 * LANES, LANES), 1)
    row2 = _iota((2 * LANES, LANES), 0)
    row2 = jnp.where(row2 < LANES, row2, row2 - LANES)
    ones2 = ((row2 < HEAD) == (lane2 < HEAD)).astype(BF16)

    kk = k * kk_ref[...]
    kk = kk / jnp.maximum(jnp.sqrt(_segsum(kk * kk, ones2)), 1e-12)
    k = k * (1.0 + (a - 1.0) * ka_ref[...])
    r_s[...] = r
    k_s[...] = k
    v_s[...] = v
    kk_s[...] = kk
    b_s[...] = kk * a
    ld_s[...] = -jnp.exp(w)

    c = WKV_CHUNK
    tri = (_iota((c, c), 0) >= _iota((c, c), 1)).astype(BF16)
    tri3 = jnp.concatenate([tri, tri, tri], axis=1)

    def chunk(ci, carry):
        rows = pl.ds(pl.multiple_of(ci * c, c), c)
        ld = ld_s[rows, :]
        hi = ld.astype(BF16)
        r1 = ld - hi.astype(F32)
        mid = r1.astype(BF16)
        lo = (r1 - mid.astype(F32)).astype(BF16)
        lc = jnp.dot(tri3, jnp.concatenate([hi, mid, lo], axis=0), preferred_element_type=F32)
        pcum = jnp.exp(lc)
        pinv = jnp.exp(-lc)
        kkc = kk_s[rows, :]
        at = -(kkc * jnp.exp(lc - ld))
        bt = b_s[rows, :] * pinv
        kt = k_s[rows, :] * pinv
        rt = r_s[rows, :] * pcum
        vc = v_s[rows, :]
        pc = pcum[c - 1:c, :]
        for pi in range(npair):
            ls = slice(pi * LANES, (pi + 1) * LANES)
            y2, s1 = _wkv_pair(at[:, ls], rt[:, ls], bt[:, ls], kt[:, ls], vc[:, ls], st_ref[pi], pc[:, ls])
            y_s[rows, ls] = y2
            st_ref[pi] = s1
        return carry

    lax.fori_loop(0, tb // c, chunk, 0)
    ns_ref[0] = st_ref[...]

    y = y_s[...]
    mu_y = _segsum(y, ones2) * (1.0 / HEAD)
    yc = y - mu_y
    var_y = _segsum(yc * yc, ones2) * (1.0 / HEAD)
    yn = yc * lax.rsqrt(var_y + GN_EPS_B) * gg_ref[...] + gb_ref[...]
    r = r_s[...]
    bonus = _segsum(r * k_s[...] * rk_ref[...], ones2) * v_s[...]
    y_ref[...] = ((yn + bonus) * gate).astype(y_ref.dtype)


def _drop_refs(body, start, count):
    def wrapped(*refs):
        return body(*refs[:start], *refs[start + count:])
    return wrapped


def _mixers(proj, lora, row0, n_streams, length, tb, st, prm, y_prev, want_v):
    t_all = proj.shape[0]
    db = prm["db"]
    nb = length // tb
    off = row0 // tb
    assert row0 % tb == 0 and length % tb == 0
    grid = (n_streams, nb)

    def rows(width, col):
        return pl.BlockSpec((tb, width), lambda s, t: (off + s * nb + t, col))

    def per_stream(shape):
        nd = len(shape)
        return pl.BlockSpec((1,) + shape, lambda s, t: (s,) + (0,) * nd)

    def whole(arr):
        nd = arr.ndim
        return pl.BlockSpec(arr.shape, lambda s, t: (0,) * nd)

    y_shape = jax.ShapeDtypeStruct((t_all, db), BF16)
    y_spec = rows(db, 0)
    nchunk = db // LANES

    def call(body, n, in_specs, args, out_specs, out_shape, scratch):
        aliases = {}
        if y_prev is not None:
            body = _drop_refs(body, len(args), 1)
            aliases = {len(args): 0}
            in_specs = in_specs + [pl.BlockSpec(memory_space=pl.ANY)]
            args = args + [y_prev[n]]
        return pl.pallas_call(
            body, grid=grid, in_specs=in_specs, out_specs=out_specs, out_shape=out_shape,
            scratch_shapes=scratch, input_output_aliases=aliases, compiler_params=_cparams(2))(*args)

    wa = prm["conv_a_w"].shape[0]
    pad_a = 32
    cw_a = prm["conv_a_w"].reshape(wa, nchunk, LANES).transpose(1, 0, 2)
    cb_a = prm["conv_a_b"].reshape(nchunk, 1, LANES)
    y_a, n_a = call(
        functools.partial(_mixer_a_kernel, tb=tb, width=wa, pad=pad_a), 0,
        [rows(db, 0), rows(db, 1), per_stream((wa - 1, db)), whole(cw_a), whole(cb_a),
         whole(prm["ln_a_g"]), whole(prm["ln_a_b"])],
        [proj, proj, st["conv_a"], cw_a, cb_a, prm["ln_a_g"], prm["ln_a_b"]],
        [y_spec, per_stream((wa - 1, db))],
        [y_shape, jax.ShapeDtypeStruct((n_streams, wa - 1, db), F32)],
        [pltpu.VMEM((nchunk, tb + pad_a, LANES), F32), pltpu.VMEM((nchunk, tb, LANES), F32)])

    pad_b = 8
    nl = lora.shape[1]
    b_w = [prm[k] for k in ("mu_rkv", "mu_l", "w0", "ww2", "a0", "wa2", "wg2", "k_k", "k_a", "r_k", "gn_g", "gn_b")]
    y_b, n_sh, n_shl, n_wkv = call(
        functools.partial(_mixer_b_kernel, tb=tb, pad=pad_b), 1,
        [rows(db, 2), rows(db, 3), rows(db, 4), pl.BlockSpec((tb, nl), lambda s, t: (off + s * nb + t, 0)),
         per_stream((1, 3 * db)), per_stream((1, nl)), per_stream((nchunk, LANES, LANES))] + [whole(w) for w in b_w],
        [proj, proj, proj, lora, st["shift_rkv"], st["shift_l"], st["wkv"]] + b_w,
        [y_spec, per_stream((1, 3 * db)), per_stream((1, nl)), per_stream((nchunk, LANES, LANES))],
        [y_shape, jax.ShapeDtypeStruct((n_streams, 1, 3 * db), F32), jax.ShapeDtypeStruct((n_streams, 1, nl), F32),
         jax.ShapeDtypeStruct((n_streams, nchunk, LANES, LANES), F32)],
        [pltpu.VMEM((tb + pad_b, 3 * db), F32), pltpu.VMEM((tb + pad_b, nl), F32),
         pltpu.VMEM((nchunk, LANES, LANES), F32)] + [pltpu.VMEM((tb, db), F32)] * 7)

    wc = prm["conv_c_w"].shape[0]
    pad_c = 8
    c_w = [prm[k] for k in ("conv_c_w", "conv_c_b", "lru_wa", "lru_ba", "lru_wx", "lru_bx", "lru_lambda")]
    y_c, n_cc, n_ch = call(
        functools.partial(_mixer_c_kernel, tb=tb, width=wc, pad=pad_c), 2,
        [rows(db, 5), rows(db, 6), per_stream((wc - 1, db)), per_stream((1, db))] + [whole(w) for w in c_w],
        [proj, proj, st["conv_c"], st["lru"]] + c_w,
        [y_spec, per_stream((wc - 1, db)), per_stream((1, db))],
        [y_shape, jax.ShapeDtypeStruct((n_streams, wc - 1, db), F32), jax.ShapeDtypeStruct((n_streams, 1, db), F32)],
        [pltpu.VMEM((tb + pad_c, db), F32)] + [pltpu.VMEM((tb, db), F32)] * 3 + [pltpu.VMEM((1, db), F32)])

    rows_d = min(tb, MLP_CHUNK)
    ws = prm["ws_pair"] if rows_d == MLP_CHUNK else prm["ws_pair_half"]
    bias = prm["bias_d"][:rows_d]
    d_out_specs, d_out_shape = [y_spec], [y_shape]
    if want_v:
        d_out_specs.append(pl.BlockSpec((tb, db), lambda s, t: (s * nb + t, 0)))
        d_out_shape.append(jax.ShapeDtypeStruct((n_streams * length, db), F32))
    d_out = call(
        functools.partial(_mixer_d_kernel, tb=tb, rows=rows_d), 3,
        [rows(db, 7), rows(db, 8), whole(prm["ln_d_g"]), whole(prm["ln_d_b"]), whole(ws), whole(bias)],
        [proj, proj, prm["ln_d_g"], prm["ln_d_b"], ws, bias],
        d_out_specs, d_out_shape, [])
    y_d = d_out[0]
    v_rows = d_out[1] if want_v else None

    new = dict(conv_a=n_a, shift_rkv=n_sh, shift_l=n_shl, wkv=n_wkv, conv_c=n_cc, lru=n_ch)
    return (y_a, y_b, y_c, y_d), new, v_rows


def _wkv_to_pairs(s):
    n, h = s.shape[0], s.shape[1]
    s = s.reshape(n, h // 2, 2, HEAD, HEAD)
    eye2 = jnp.eye(2, dtype=s.dtype)
    z = s[:, :, :, :, None, :] * eye2[None, None, :, None, :, None]
    return z.reshape(n, h // 2, 2 * HEAD, 2 * HEAD)


def _wkv_from_pairs(z):
    n, hp = z.shape[0], z.shape[1]
    z = z.reshape(n, hp, 2, HEAD, 2, HEAD)
    s = jnp.stack([z[:, :, 0, :, 0, :], z[:, :, 1, :, 1, :]], axis=2)
    return s.reshape(n, 2 * hp, HEAD, HEAD)


def _row(v):
    return v.reshape(1, -1)


def kernel(x_prompt, x_sample, state_conv_a, state_shift_b, state_wkv_b, state_conv_c, state_lru_c, ln_in_g, ln_in_b, w_in, conv_a_w, conv_a_b, ln_a_g, ln_a_b, mu_b, w0_b, w_w2_b, a0_b, w_a2_b, w_g2_b, k_k_b, k_a_b, r_k_b, gn_b_g, gn_b_b, conv_c_w, conv_c_b, lru_wa, lru_ba, lru_wx, lru_bx, lru_lambda, ln_d_g, ln_d_b, w_s_d, b_s_d, w_branch, w_out, ln1_g, ln1_b, w_up, w_down, ln2_g, ln2_b):
    batch, seq, d = x_prompt.shape
    dec_batch, dec_seq, _ = x_sample.shape
    depth = w_in.shape[0]
    db = d // 4
    n_heads = db // HEAD
    lw, la, lg = w_w2_b.shape[1], w_a2_b.shape[1], w_g2_b.shape[1]
    assert db % LANES == 0 and lw == HEAD and la == HEAD and lg == LANES
    assert w_s_d.shape[1] == db // HEAD and w_s_d.shape[2] == MLP_CHUNK
    assert dec_seq == STREAM_CHUNK
    tb_p = 256
    assert seq % tb_p == 0
    n_p, n_s = batch * seq, dec_batch * dec_seq
    alpha = (2 * depth) ** 0.25
    off_b = 2 * db
    off_l = off_b + 3 * db
    off_c = off_l + lw + la + lg
    off_d = off_c + 2 * db
    off_g = off_d + 2 * db

    x = jnp.concatenate([x_prompt.reshape(n_p, d), x_sample.reshape(n_s, d)], axis=0)
    x, xb = _ln_call(x, None, ln_in_g, ln_in_b, 1.0, 256)

    zeros_p = dict(
        conv_a=jnp.zeros((batch, conv_a_w.shape[1] - 1, db), F32),
        shift_rkv=jnp.zeros((batch, 1, 3 * db), F32),
        shift_l=jnp.zeros((batch, 1, lw + la + lg), F32),
        wkv=jnp.zeros((batch, n_heads // 2, LANES, LANES), F32),
        conv_c=jnp.zeros((batch, conv_c_w.shape[1] - 1, db), F32),
        lru=jnp.zeros((batch, 1, db), F32))

    blk = jnp.arange(MLP_CHUNK) // STREAM_CHUNK
    outs_p, outs_s, v_rows_s = [], [], []
    for l in range(depth):
        wl = w_in[l]
        w_mix = jnp.concatenate([wl[:, :off_l], wl[:, off_c:off_g]], axis=1).astype(BF16)
        w_lora = wl[:, off_l:off_c].astype(BF16)
        w_gate = wl[:, off_g:].astype(BF16)
        proj = _mm(xb, w_mix, F32, 1024, 512)
        lora = _mm(xb, w_lora, F32, 1024, lw + la + lg)
        gates = _mm(xb, w_gate, BF16, 1024, 1024)

        ws = jnp.where(blk[:, None] >= blk[None, :], w_s_d[l], 0.0).astype(BF16)
        half = STREAM_CHUNK
        zpad = jnp.zeros((HEAD, db), F32)
        seg = (jnp.arange(LANES)[:, None] // HEAD == jnp.arange(LANES)[None, :] // HEAD)
        prm = dict(
            db=db,
            conv_a_w=conv_a_w[l], conv_a_b=conv_a_b[l], ln_a_g=_row(ln_a_g[l]), ln_a_b=_row(ln_a_b[l]),
            mu_rkv=_row(mu_b[l][:3 * db]), mu_l=_row(mu_b[l][3 * db:]), w0=_row(w0_b[l]),
            ww2=_rhs3(jnp.concatenate([w_w2_b[l], zpad], axis=0)), a0=_row(a0_b[l]),
            wa2=_rhs3(jnp.concatenate([zpad, w_a2_b[l]], axis=0)), wg2=_rhs3(w_g2_b[l]),
            k_k=_row(k_k_b[l]), k_a=_row(k_a_b[l]), r_k=_row(r_k_b[l]), gn_g=_row(gn_b_g[l]), gn_b=_row(gn_b_b[l]),
            conv_c_w=conv_c_w[l], conv_c_b=_row(conv_c_b[l]),
            lru_wa=jax.scipy.linalg.block_diag(*lru_wa[l]).astype(BF16), lru_ba=_row(lru_ba[l]),
            lru_wx=jax.scipy.linalg.block_diag(*lru_wx[l]).astype(BF16), lru_bx=_row(lru_bx[l]),
            lru_lambda=_row(lru_lambda[l]),
            ln_d_g=_row(ln_d_g[l]), ln_d_b=_row(ln_d_b[l]),
            ws_pair=jnp.concatenate([ws[0::2], ws[1::2]], axis=2),
            ws_pair_half=jnp.concatenate([ws[0::2, :half, :half], ws[1::2, :half, :half]], axis=2),
            bias_d=jnp.repeat(b_s_d[l].T, HEAD, axis=1),
        )
        st_s = dict(
            conv_a=state_conv_a[l], shift_rkv=state_shift_b[l][:, None, :3 * db],
            shift_l=state_shift_b[l][:, None, 3 * db:], wkv=_wkv_to_pairs(state_wkv_b[l]),
            conv_c=state_conv_c[l], lru=state_lru_c[l][:, None, :])

        ys_p, new_p, _ = _mixers(proj, lora, 0, batch, seq, tb_p, zeros_p, prm, None, False)
        ys, new_s, v_rows = _mixers(proj, lora, n_p, dec_batch, dec_seq, dec_seq, st_s, prm, ys_p, True)
        merged = _merge(ys, gates, w_branch[l].astype(BF16), 512, 1024)
        o = _mm(merged, w_out[l].astype(BF16), F32, 1024, 512)
        x, xb = _ln_call(x, o, ln1_g[l], ln1_b[l], alpha, 256)
        hdn = _mm(xb, w_up[l].astype(BF16), BF16, 1024, 1024, act="relu2")
        o = _mm_acc(hdn, w_down[l].astype(BF16), F32, 1024, 1024, 2048)
        x, xb = _ln_call(x, o, ln2_g[l], ln2_b[l], alpha, 256)
        outs_p.append(new_p)
        outs_s.append(new_s)
        v_rows_s.append(v_rows.reshape(dec_batch, dec_seq, db))

    def collect(outs):
        conv_a = jnp.stack([o["conv_a"] for o in outs])
        shift = jnp.stack([jnp.concatenate([o["shift_rkv"][:, 0], o["shift_l"][:, 0]], axis=1) for o in outs])
        wkv = jnp.stack([_wkv_from_pairs(o["wkv"]) for o in outs])
        conv_c = jnp.stack([o["conv_c"] for o in outs])
        lru = jnp.stack([o["lru"][:, 0] for o in outs])
        return conv_a, shift, wkv, conv_c, lru

    y_prompt = x[:n_p].reshape(batch, seq, d)
    y_sample = x[n_p:].reshape(dec_batch, dec_seq, d)
    return (y_prompt, y_sample) + collect(outs_p) + collect(outs_s) + (jnp.stack(v_rows_s),)
```

```python
import functools
import math

import jax
import jax.numpy as jnp
from jax import lax
from jax.experimental import pallas as pl
from jax.experimental.pallas import tpu as pltpu

F32 = jnp.float32
BF16 = jnp.bfloat16

LANES = 128
HEAD = 64
WKV_CHUNK = 64
MLP_CHUNK = 128
STREAM_CHUNK = 64
WCOL = 256
VMEM_LIMIT = 56 * 1024 * 1024
LN_EPS = 1e-5
GN_EPS_B = 64e-5
LRU_C = 8.0


def _cparams(n_axes):
    return pltpu.CompilerParams(dimension_semantics=("arbitrary",) * n_axes, vmem_limit_bytes=VMEM_LIMIT)


def _dot(a, b, dims=(((1,), (0,)), ((), ()))):
    return lax.dot_general(a.astype(BF16), b.astype(BF16), dims, preferred_element_type=F32)


_NT = (((1,), (1,)), ((), ()))


def _hi_lo(x):
    hi = x.astype(BF16)
    lo = (x - hi.astype(F32)).astype(BF16)
    return hi, lo


def _lhs3(x):
    hi, lo = _hi_lo(x)
    return jnp.concatenate([hi, hi, lo], axis=1)


def _rhs3(w):
    hi, lo = _hi_lo(w.astype(F32))
    return jnp.concatenate([hi, lo, hi], axis=0)


def _gelu(x):
    return 0.5 * x * (1.0 + jnp.tanh(math.sqrt(2.0 / math.pi) * (x + 0.044715 * (x * x * x))))


def _sigmoid(x):
    return 1.0 / (1.0 + jnp.exp(-x))


def _softplus(x):
    return jnp.maximum(x, 0.0) + jnp.log1p(jnp.exp(-jnp.abs(x)))


def _layer_norm(x, g, b, eps=LN_EPS):
    mu = jnp.mean(x, axis=-1, keepdims=True)
    xc = x - mu
    var = jnp.mean(xc * xc, axis=-1, keepdims=True)
    return xc * lax.rsqrt(var + eps) * g + b


def _iota(shape, dim):
    return lax.broadcasted_iota(jnp.int32, shape, dim)


def _tile(extent, target, quantum=LANES):
    best = None
    for cand in range(quantum, min(extent, target) + 1, quantum):
        if extent % cand == 0:
            best = cand
    assert best is not None, (extent, target, quantum)
    return best


def _drop_refs(body, start, count):
    def wrapped(*refs):
        return body(*refs[:start], *refs[start + count:])
    return wrapped


def _mm_w32_kernel(a_ref, *refs, nw, cw, act):
    w_refs, o_ref, wb_ref = refs[:nw], refs[nw], refs[nw + 1]

    @pl.when(pl.program_id(1) == 0)
    def _():
        for q in range(nw):
            wb_ref[:, q * cw:(q + 1) * cw] = w_refs[q][...].astype(BF16)

    acc = jnp.dot(a_ref[...], wb_ref[...], preferred_element_type=F32)
    if act == "relu2":
        acc = jnp.square(jnp.maximum(acc, 0.0))
    o_ref[...] = acc.astype(o_ref.dtype)


def _mm_w32(a, w, layer, n_blocks, src_block, cw, tn, tm, out_dtype, act=None, name="mm"):
    m, k = a.shape
    tm = _tile(m, tm)
    nw = tn // cw
    assert n_blocks % nw == 0 and w.shape[1] == k
    w_specs = [pl.BlockSpec((None, k, cw), functools.partial(lambda j, i, q: (layer, 0, src_block(nw * j + q)), q=q))
               for q in range(nw)]
    return pl.pallas_call(
        functools.partial(_mm_w32_kernel, nw=nw, cw=cw, act=act),
        grid=(n_blocks // nw, m // tm),
        in_specs=[pl.BlockSpec((tm, k), lambda j, i: (i, 0))] + w_specs,
        out_specs=pl.BlockSpec((tm, tn), lambda j, i: (i, j)),
        out_shape=jax.ShapeDtypeStruct((m, n_blocks * cw), out_dtype),
        scratch_shapes=[pltpu.VMEM((k, tn), BF16)],
        compiler_params=_cparams(2), name=name,
    )(a, *([w] * nw))


def _mm_acc_kernel(a_ref, w_ref, o_ref, acc_ref):
    kk = pl.program_id(2)

    @pl.when(kk == 0)
    def _():
        acc_ref[...] = jnp.zeros_like(acc_ref)

    acc_ref[...] += jnp.dot(a_ref[...], w_ref[...], preferred_element_type=F32)

    @pl.when(kk == pl.num_programs(2) - 1)
    def _():
        o_ref[...] = acc_ref[...].astype(o_ref.dtype)


def _mm_acc(a, w, out_dtype, tm, tn, tk, name="mm_acc"):
    m, k = a.shape
    n = w.shape[1]
    tm, tn, tk = _tile(m, tm), _tile(n, tn), _tile(k, tk)
    return pl.pallas_call(
        _mm_acc_kernel,
        grid=(m // tm, n // tn, k // tk),
        in_specs=[pl.BlockSpec((tm, tk), lambda i, j, q: (i, q)),
                  pl.BlockSpec((tk, tn), lambda i, j, q: (q, j))],
        out_specs=pl.BlockSpec((tm, tn), lambda i, j, q: (i, j)),
        out_shape=jax.ShapeDtypeStruct((m, n), out_dtype),
        scratch_shapes=[pltpu.VMEM((tm, tn), F32)],
        compiler_params=_cparams(3), name=name,
    )(a, w)


def _ln_kernel(*refs, alpha, has_res, n_out):
    x_ref = refs[0]
    r_ref = refs[1] if has_res else None
    g_ref, b_ref = refs[1 + has_res], refs[2 + has_res]
    outs = refs[3 + has_res:3 + has_res + n_out]
    x = x_ref[...]
    if has_res:
        x = alpha * x + r_ref[...]
    y = _layer_norm(x, g_ref[...], b_ref[...])
    for o_ref in outs:
        o_ref[...] = y.astype(o_ref.dtype)


def _ln_rows(x, res, g, b, alpha, tm, in_row0, nrows, out_rows, out_row0, out_dtypes, prev=None, name="ln"):
    d = x.shape[1]
    tm = _tile(nrows, tm, 8)
    assert in_row0 % tm == 0 and out_row0 % tm == 0
    ib, ob = in_row0 // tm, out_row0 // tm
    row_in = pl.BlockSpec((tm, d), lambda i: (ib + i, 0))
    row_out = pl.BlockSpec((tm, d), lambda i: (ob + i, 0))
    vec = pl.BlockSpec((1, d), lambda i: (0, 0))
    has_res = res is not None
    args = [x] + ([res] if has_res else []) + [g.reshape(1, d), b.reshape(1, d)]
    specs = [row_in] * (1 + has_res) + [vec, vec]
    body = functools.partial(_ln_kernel, alpha=alpha, has_res=int(has_res), n_out=len(out_dtypes))
    aliases = {}
    if prev is not None:
        body = _drop_refs(body, len(args), len(prev))
        aliases = {len(args) + n: n for n in range(len(prev))}
        specs = specs + [pl.BlockSpec(memory_space=pl.ANY)] * len(prev)
        args = args + list(prev)
    return pl.pallas_call(
        body, grid=(nrows // tm,), in_specs=specs, out_specs=[row_out] * len(out_dtypes),
        out_shape=[jax.ShapeDtypeStruct((out_rows, d), dt) for dt in out_dtypes],
        input_output_aliases=aliases, compiler_params=_cparams(1), name=name,
    )(*args)


def _merge_kernel(ya_ref, yb_ref, yc_ref, yd_ref, g0_ref, g1_ref, g2_ref, g3_ref, wb_ref, o_ref):
    acc = None
    for n, (y_ref, g_ref) in enumerate(((ya_ref, g0_ref), (yb_ref, g1_ref), (yc_ref, g2_ref), (yd_ref, g3_ref))):
        br = jnp.dot(y_ref[...], wb_ref[n], preferred_element_type=F32)
        term = _sigmoid(g_ref[...].astype(F32)) * br
        acc = term if acc is None else acc + term
    o_ref[...] = acc.astype(o_ref.dtype)


def _merge(ys, gates, wb, tm, tn):
    m, db = ys[0].shape
    d = wb.shape[2]
    tm, tn = _tile(m, tm), _tile(d, tn)
    nj = d // tn
    y_spec = pl.BlockSpec((tm, db), lambda j, i: (i, 0))
    g_specs = [pl.BlockSpec((tm, tn), functools.partial(lambda j, i, n: (i, n * nj + j), n=n)) for n in range(4)]
    return pl.pallas_call(
        _merge_kernel,
        grid=(nj, m // tm),
        in_specs=[y_spec] * 4 + g_specs + [pl.BlockSpec((4, db, tn), lambda j, i: (0, 0, j))],
        out_specs=pl.BlockSpec((tm, tn), lambda j, i: (i, j)),
        out_shape=jax.ShapeDtypeStruct((m, d), BF16),
        compiler_params=_cparams(2), name="merge",
    )(*ys, gates, gates, gates, gates, wb)


def _mixer_a_kernel(a1_ref, a2_ref, hist_ref, cw_ref, cb_ref, g_ref, b_ref, y_ref, nh_ref, zext_ref, acc_ref,
                    *, tb, width, pad):
    hist = width - 1
    nchunk = zext_ref.shape[0]
    t = pl.program_id(1)

    @pl.when(t == 0)
    def _():
        h = hist_ref[0]
        for c in range(nchunk):
            zext_ref[c, 0:pad - hist, :] = jnp.zeros((pad - hist, LANES), F32)
            zext_ref[c, pad - hist:pad, :] = h[:, c * LANES:(c + 1) * LANES]

    z = a1_ref[...] * _sigmoid(a2_ref[...])
    for c in range(nchunk):
        zext_ref[c, pad:pad + tb, :] = z[:, c * LANES:(c + 1) * LANES]

    rc = 64

    def conv_chunk(c, carry):
        w = cw_ref[c]
        bias = cb_ref[c]
        for r0 in range(0, tb, rc):
            acc = jnp.broadcast_to(bias, (rc, LANES))
            for j in range(width):
                acc = acc + w[j:j + 1, :] * zext_ref[c, pl.ds(pad - hist + j + r0, rc), :]
            acc_ref[c, r0:r0 + rc, :] = acc
        return carry

    lax.fori_loop(0, nchunk, conv_chunk, 0)
    y = jnp.concatenate([acc_ref[c] for c in range(nchunk)], axis=1)
    y = _layer_norm(y, g_ref[...], b_ref[...])
    y_ref[...] = (y * _sigmoid(y)).astype(y_ref.dtype)
    nh_ref[0] = jnp.concatenate([zext_ref[c, tb + pad - hist:tb + pad, :] for c in range(nchunk)], axis=1)
    for c in range(nchunk):
        zext_ref[c, 0:pad, :] = zext_ref[c, tb:tb + pad, :]


def _mixer_c_kernel(xb_ref, gate_ref, hist_ref, h0_ref, cw_ref, cb_ref, wa_ref, ba_ref, wx_ref, bx_ref, lam_ref,
                    y_ref, nhist_ref, nh_ref, xext_ref, a_ref, u_ref, hs_ref, hcar_ref, *, tb, width, pad):
    hist = width - 1
    t = pl.program_id(1)
    db = xb_ref.shape[1]

    @pl.when(t == 0)
    def _():
        xext_ref[0:pad - hist, :] = jnp.zeros((pad - hist, db), F32)
        xext_ref[pad - hist:pad, :] = hist_ref[0]
        hcar_ref[...] = h0_ref[0]

    xext_ref[pad:pad + tb, :] = xb_ref[...]
    cw = cw_ref[...]
    xc = jnp.broadcast_to(cb_ref[...], (tb, db))
    for j in range(width):
        xc = xc + cw[j:j + 1, :] * xext_ref[pad - hist + j:pad - hist + j + tb, :]
    xcb = xc.astype(BF16)
    r = _sigmoid(jnp.dot(xcb, wa_ref[...], preferred_element_type=F32) + ba_ref[...])
    i = _sigmoid(jnp.dot(xcb, wx_ref[...], preferred_element_type=F32) + bx_ref[...])
    log_a = (LRU_C * r) * (-_softplus(-lam_ref[...]))
    a_ref[...] = jnp.exp(log_a)
    u_ref[...] = jnp.sqrt(1.0 - jnp.exp(2.0 * log_a)) * (i * xc)

    def step(k, h):
        h = a_ref[pl.ds(k, 1), :] * h + u_ref[pl.ds(k, 1), :]
        hs_ref[pl.ds(k, 1), :] = h
        return h

    h = lax.fori_loop(0, tb, step, hcar_ref[...], unroll=8)
    hcar_ref[...] = h
    y_ref[...] = (hs_ref[...] * _gelu(gate_ref[...])).astype(y_ref.dtype)
    nhist_ref[0] = xext_ref[tb + pad - hist:tb + pad, :]
    nh_ref[0] = h
    xext_ref[0:pad, :] = xext_ref[tb:tb + pad, :]


def _mixer_d_kernel(u_ref, v_ref, g_ref, b_ref, ws_ref, bias_ref, y_ref, *maybe_vrows_ref, tb, rows):
    v = _layer_norm(_gelu(v_ref[...]), g_ref[...], b_ref[...])
    for vrows_ref in maybe_vrows_ref:
        vrows_ref[...] = v
    vb = v.astype(BF16)
    npair = v.shape[1] // LANES
    first = _iota((rows, LANES), 1) < HEAD
    zero = jnp.zeros((rows, LANES), BF16)
    for r0 in range(0, tb, rows):
        for p in range(npair):
            ls = slice(p * LANES, (p + 1) * LANES)
            v2 = vb[r0:r0 + rows, ls]
            vbd = jnp.concatenate([jnp.where(first, v2, zero), jnp.where(first, zero, v2)], axis=0)
            s = jnp.dot(ws_ref[p], vbd, preferred_element_type=F32) + bias_ref[:, ls]
            y_ref[r0:r0 + rows, ls] = (_gelu(u_ref[r0:r0 + rows, ls]) * s).astype(y_ref.dtype)


def _segsum(x, ones2):
    outs = []
    for c in range(x.shape[1] // LANES):
        hi, lo = _hi_lo(x[:, c * LANES:(c + 1) * LANES])
        outs.append(jnp.dot(jnp.concatenate([hi, lo], axis=1), ones2, preferred_element_type=F32))
    return jnp.concatenate(outs, axis=1)


def _wkv_chunk(rows, at_c, rt_c, bt_c, kt_c, v_s, pc_c, st_ref, y_s, ap_ref, x_ref, ak_ref, ar_ref, rhs_ref, u_ref):
    c = WKV_CHUNK
    npair = st_ref.shape[0]
    lane = _iota((c, LANES), 1)
    first = lane < HEAD

    def bd(x):
        return jnp.concatenate([jnp.where(first, x, 0.0), jnp.where(first, 0.0, x)], axis=0)

    row = _iota((c, LANES), 0)
    col = jnp.where(first, lane, lane - HEAD)
    strict = row > col
    incl = row >= col
    eye = (_iota((LANES, LANES), 0) == _iota((LANES, LANES), 1)).astype(F32)
    same_head = (_iota((LANES, LANES), 0) < HEAD) == (_iota((LANES, LANES), 1) < HEAD)
    lanes = [slice(p * LANES, (p + 1) * LANES) for p in range(npair)]

    for p, ls in enumerate(lanes):
        g = _dot(jnp.concatenate([at_c[:, ls], rt_c[:, ls]], axis=0),
                 jnp.concatenate([bd(bt_c[:, ls]), bd(kt_c[:, ls])], axis=0), _NT)
        a_pow = bd(jnp.where(strict, g[:c, :LANES], 0.0))
        ap_ref[p] = a_pow.astype(BF16)
        x_ref[p] = eye + a_pow
        ak_ref[p] = jnp.where(strict, g[:c, LANES:], 0.0).astype(BF16)
        ar_ref[p] = jnp.concatenate([jnp.where(incl, g[c:, :LANES], 0.0),
                                     jnp.where(incl, g[c:, LANES:], 0.0)], axis=1).astype(BF16)

    for _ in range(5):
        for p in range(npair):
            a_pow = ap_ref[p]
            ap_ref[p] = jnp.dot(a_pow, a_pow, preferred_element_type=F32).astype(BF16)
        for p in range(npair):
            x = x_ref[p]
            x_ref[p] = x + jnp.dot(x.astype(BF16), ap_ref[p], preferred_element_type=F32)

    for p, ls in enumerate(lanes):
        rhs_ref[p] = _dot(at_c[:, ls], st_ref[p], _NT) + jnp.dot(ak_ref[p], bd(v_s[rows, ls]).astype(BF16),
                                                                  preferred_element_type=F32)
    for p in range(npair):
        x = x_ref[p]
        u_ref[p] = _dot(x[:c] + x[c:], bd(rhs_ref[p]))
    for p, ls in enumerate(lanes):
        u2 = u_ref[p]
        v2 = v_s[rows, ls]
        s0 = st_ref[p]
        y2 = _dot(rt_c[:, ls], s0, _NT) + jnp.dot(ar_ref[p], jnp.concatenate([bd(u2), bd(v2)], axis=0).astype(BF16),
                                                  preferred_element_type=F32)
        uvt = jnp.transpose(jnp.concatenate([u2, v2], axis=0))
        upd = _dot(uvt, jnp.concatenate([bt_c[:, ls], kt_c[:, ls]], axis=0))
        y_s[rows, ls] = y2
        st_ref[p] = jnp.where(same_head, (s0 + upd) * pc_c[:, ls], 0.0)


def _mixer_b_kernel(pr_ref, pk_ref, pv_ref, pl_ref, sh_ref, shl_ref, s0_ref,
                    mu_ref, mul_ref, w0_ref, ww2_ref, a0_ref, wa2_ref, wg2_ref, kk_ref, ka_ref, rk_ref, gg_ref, gb_ref,
                    y_ref, nsh_ref, nshl_ref, ns_ref,
                    pext_ref, lext_ref, st_ref, r_s, k_s, v_s, kk_s, b_s, ld_s, y_s,
                    at_c, rt_c, bt_c, kt_c, pc_c, ap_ref, x_ref, ak_ref, ar_ref, rhs_ref, u_ref, *, tb, pad):
    t = pl.program_id(1)
    db = pr_ref.shape[1]

    @pl.when(t == 0)
    def _():
        pext_ref[0:pad, :] = jnp.broadcast_to(sh_ref[0], (pad, 3 * db))
        lext_ref[0:pad, :] = jnp.broadcast_to(shl_ref[0], (pad, lext_ref.shape[1]))
        st_ref[...] = s0_ref[0]

    pext_ref[pad:pad + tb, 0:db] = pr_ref[...]
    pext_ref[pad:pad + tb, db:2 * db] = pk_ref[...]
    pext_ref[pad:pad + tb, 2 * db:3 * db] = pv_ref[...]
    lext_ref[pad:pad + tb, :] = pl_ref[...]
    p = pext_ref[pad:pad + tb, :]
    ps = p + mu_ref[...] * (pext_ref[pad - 1:pad - 1 + tb, :] - p)
    lo_ = lext_ref[pad:pad + tb, :]
    lo_s = lo_ + mul_ref[...] * (lext_ref[pad - 1:pad - 1 + tb, :] - lo_)
    nsh_ref[0] = pext_ref[pad + tb - 1:pad + tb, :]
    nshl_ref[0] = lext_ref[pad + tb - 1:pad + tb, :]
    pext_ref[0:pad, :] = pext_ref[tb:tb + pad, :]
    lext_ref[0:pad, :] = lext_ref[tb:tb + pad, :]

    r = ps[:, 0:db]
    k = ps[:, db:2 * db]
    v = ps[:, 2 * db:3 * db]
    wa_lo = lo_s[:, 0:LANES]
    g_lo = lo_s[:, LANES:2 * LANES]
    w = -_softplus(-(w0_ref[...] + jnp.dot(_lhs3(jnp.tanh(wa_lo)), ww2_ref[...], preferred_element_type=F32))) - 0.5
    a = _sigmoid(a0_ref[...] + jnp.dot(_lhs3(wa_lo), wa2_ref[...], preferred_element_type=F32))
    gate = jnp.dot(_lhs3(_sigmoid(g_lo)), wg2_ref[...], preferred_element_type=F32)

    lane2 = _iota((2 * LANES, LANES), 1)
    row2 = _iota((2 * LANES, LANES), 0)
    row2 = jnp.where(row2 < LANES, row2, row2 - LANES)
    ones2 = ((row2 < HEAD) == (lane2 < HEAD)).astype(BF16)

    kk = k * kk_ref[...]
    kk = kk / jnp.maximum(jnp.sqrt(_segsum(kk * kk, ones2)), 1e-12)
    k = k * (1.0 + (a - 1.0) * ka_ref[...])
    r_s[...] = r
    k_s[...] = k
    v_s[...] = v
    kk_s[...] = kk
    b_s[...] = kk * a
    ld_s[...] = -jnp.exp(w)

    c = WKV_CHUNK
    tri = (_iota((c, c), 0) >= _iota((c, c), 1)).astype(BF16)
    tri3 = jnp.concatenate([tri, tri, tri], axis=1)

    def chunk(ci, carry):
        rows = pl.ds(pl.multiple_of(ci * c, c), c)
        ld = ld_s[rows, :]
        hi = ld.astype(BF16)
        r1 = ld - hi.astype(F32)
        mid = r1.astype(BF16)
        lo = (r1 - mid.astype(F32)).astype(BF16)
        lc = jnp.dot(tri3, jnp.concatenate([hi, mid, lo], axis=0), preferred_element_type=F32)
        pcum = jnp.exp(lc)
        pinv = jnp.exp(-lc)
        at_c[...] = -(kk_s[rows, :] * jnp.exp(lc - ld))
        bt_c[...] = b_s[rows, :] * pinv
        kt_c[...] = k_s[rows, :] * pinv
        rt_c[...] = r_s[rows, :] * pcum
        pc_c[...] = pcum[c - 1:c, :]
        _wkv_chunk(rows, at_c, rt_c, bt_c, kt_c, v_s, pc_c, st_ref, y_s, ap_ref, x_ref, ak_ref, ar_ref, rhs_ref, u_ref)
        return carry

    lax.fori_loop(0, tb // c, chunk, 0)
    ns_ref[0] = st_ref[...]

    y = y_s[...]
    mu_y = _segsum(y, ones2) * (1.0 / HEAD)
    yc = y - mu_y
    var_y = _segsum(yc * yc, ones2) * (1.0 / HEAD)
    yn = yc * lax.rsqrt(var_y + GN_EPS_B) * gg_ref[...] + gb_ref[...]
    bonus = _segsum(r_s[...] * k_s[...] * rk_ref[...], ones2) * v_s[...]
    y_ref[...] = ((yn + bonus) * gate).astype(y_ref.dtype)


def _mixers(proj, lora, row0, n_streams, length, tb, st, prm, y_prev, want_v, tag):
    t_all = proj.shape[0]
    db = prm["db"]
    nb = length // tb
    off = row0 // tb
    assert row0 % tb == 0 and length % tb == 0
    grid = (n_streams, nb)

    def rows(width, col):
        return pl.BlockSpec((tb, width), lambda s, t: (off + s * nb + t, col))

    def per_stream(shape):
        nd = len(shape)
        return pl.BlockSpec((1,) + shape, lambda s, t: (s,) + (0,) * nd)

    def whole(arr):
        nd = arr.ndim
        return pl.BlockSpec(arr.shape, lambda s, t: (0,) * nd)

    y_shape = jax.ShapeDtypeStruct((t_all, db), BF16)
    y_spec = rows(db, 0)
    nchunk = db // LANES

    def call(body, n, in_specs, args, out_specs, out_shape, scratch, name):
        aliases = {}
        if y_prev is not None:
            body = _drop_refs(body, len(args), 1)
            aliases = {len(args): 0}
            in_specs = in_specs + [pl.BlockSpec(memory_space=pl.ANY)]
            args = args + [y_prev[n]]
        return pl.pallas_call(
            body, grid=grid, in_specs=in_specs, out_specs=out_specs, out_shape=out_shape, scratch_shapes=scratch,
            input_output_aliases=aliases, compiler_params=_cparams(2), name=name + tag)(*args)

    wa = prm["conv_a_w"].shape[0]
    pad_a = 32
    cw_a = prm["conv_a_w"].reshape(wa, nchunk, LANES).transpose(1, 0, 2)
    cb_a = prm["conv_a_b"].reshape(nchunk, 1, LANES)
    y_a, n_a = call(
        functools.partial(_mixer_a_kernel, tb=tb, width=wa, pad=pad_a), 0,
        [rows(db, 0), rows(db, 1), per_stream((wa - 1, db)), whole(cw_a), whole(cb_a),
         whole(prm["ln_a_g"]), whole(prm["ln_a_b"])],
        [proj, proj, st["conv_a"], cw_a, cb_a, prm["ln_a_g"], prm["ln_a_b"]],
        [y_spec, per_stream((wa - 1, db))],
        [y_shape, jax.ShapeDtypeStruct((n_streams, wa - 1, db), F32)],
        [pltpu.VMEM((nchunk, tb + pad_a, LANES), F32), pltpu.VMEM((nchunk, tb, LANES), F32)], "mixer_a")

    pad_b = 8
    nl = lora.shape[1]
    c = WKV_CHUNK
    b_w = [prm[k] for k in ("mu_rkv", "mu_l", "w0", "ww2", "a0", "wa2", "wg2", "k_k", "k_a", "r_k", "gn_g", "gn_b")]
    y_b, n_sh, n_shl, n_wkv = call(
        functools.partial(_mixer_b_kernel, tb=tb, pad=pad_b), 1,
        [rows(db, 2), rows(db, 3), rows(db, 4), pl.BlockSpec((tb, nl), lambda s, t: (off + s * nb + t, 0)),
         per_stream((1, 3 * db)), per_stream((1, nl)), per_stream((nchunk, LANES, LANES))] + [whole(w) for w in b_w],
        [proj, proj, proj, lora, st["shift_rkv"], st["shift_l"], st["wkv"]] + b_w,
        [y_spec, per_stream((1, 3 * db)), per_stream((1, nl)), per_stream((nchunk, LANES, LANES))],
        [y_shape, jax.ShapeDtypeStruct((n_streams, 1, 3 * db), F32), jax.ShapeDtypeStruct((n_streams, 1, nl), F32),
         jax.ShapeDtypeStruct((n_streams, nchunk, LANES, LANES), F32)],
        [pltpu.VMEM((tb + pad_b, 3 * db), F32), pltpu.VMEM((tb + pad_b, nl), F32),
         pltpu.VMEM((nchunk, LANES, LANES), F32)] + [pltpu.VMEM((tb, db), F32)] * 7
        + [pltpu.VMEM((c, db), F32)] * 4 + [pltpu.VMEM((1, db), F32),
                                            pltpu.VMEM((nchunk, LANES, LANES), BF16),
                                            pltpu.VMEM((nchunk, LANES, LANES), F32),
                                            pltpu.VMEM((nchunk, c, LANES), BF16),
                                            pltpu.VMEM((nchunk, c, 2 * LANES), BF16),
                                            pltpu.VMEM((nchunk, c, LANES), F32),
                                            pltpu.VMEM((nchunk, c, LANES), F32)], "mixer_b")

    wc = prm["conv_c_w"].shape[0]
    pad_c = 8
    c_w = [prm[k] for k in ("conv_c_w", "conv_c_b", "lru_wa", "lru_ba", "lru_wx", "lru_bx", "lru_lambda")]
    y_c, n_cc, n_ch = call(
        functools.partial(_mixer_c_kernel, tb=tb, width=wc, pad=pad_c), 2,
        [rows(db, 5), rows(db, 6), per_stream((wc - 1, db)), per_stream((1, db))] + [whole(w) for w in c_w],
        [proj, proj, st["conv_c"], st["lru"]] + c_w,
        [y_spec, per_stream((wc - 1, db)), per_stream((1, db))],
        [y_shape, jax.ShapeDtypeStruct((n_streams, wc - 1, db), F32), jax.ShapeDtypeStruct((n_streams, 1, db), F32)],
        [pltpu.VMEM((tb + pad_c, db), F32)] + [pltpu.VMEM((tb, db), F32)] * 3 + [pltpu.VMEM((1, db), F32)], "mixer_c")

    rows_d = min(tb, MLP_CHUNK)
    ws = prm["ws_pair"] if rows_d == MLP_CHUNK else prm["ws_pair_half"]
    bias = prm["bias_d"][:rows_d]
    d_out_specs, d_out_shape = [y_spec], [y_shape]
    if want_v:
        d_out_specs.append(pl.BlockSpec((tb, db), lambda s, t: (s * nb + t, 0)))
        d_out_shape.append(jax.ShapeDtypeStruct((n_streams * length, db), F32))
    d_out = call(
        functools.partial(_mixer_d_kernel, tb=tb, rows=rows_d), 3,
        [rows(db, 7), rows(db, 8), whole(prm["ln_d_g"]), whole(prm["ln_d_b"]), whole(ws), whole(bias)],
        [proj, proj, prm["ln_d_g"], prm["ln_d_b"], ws, bias],
        d_out_specs, d_out_shape, [], "mixer_d")
    y_d = d_out[0]
    v_rows = d_out[1] if want_v else None

    new = dict(conv_a=n_a, shift_rkv=n_sh, shift_l=n_shl, wkv=n_wkv, conv_c=n_cc, lru=n_ch)
    return (y_a, y_b, y_c, y_d), new, v_rows


def _wkv_to_pairs(s):
    n, h = s.shape[0], s.shape[1]
    s = s.reshape(n, h // 2, 2, HEAD, HEAD)
    eye2 = jnp.eye(2, dtype=s.dtype)
    z = s[:, :, :, :, None, :] * eye2[None, None, :, None, :, None]
    return z.reshape(n, h // 2, 2 * HEAD, 2 * HEAD)


def _wkv_from_pairs(z):
    n, hp = z.shape[0], z.shape[1]
    z = z.reshape(n, hp, 2, HEAD, 2, HEAD)
    s = jnp.stack([z[:, :, 0, :, 0, :], z[:, :, 1, :, 1, :]], axis=2)
    return s.reshape(n, 2 * hp, HEAD, HEAD)


def _block_diag(w):
    n, b, _ = w.shape
    tiled = jnp.tile(w.reshape(n * b, b), (1, n))
    same = (jnp.arange(n * b)[:, None] // b) == (jnp.arange(n * b)[None, :] // b)
    return jnp.where(same, tiled, 0.0)


def _row(v):
    return v.reshape(1, -1)


def kernel(x_prompt, x_sample, state_conv_a, state_shift_b, state_wkv_b, state_conv_c, state_lru_c, ln_in_g, ln_in_b, w_in, conv_a_w, conv_a_b, ln_a_g, ln_a_b, mu_b, w0_b, w_w2_b, a0_b, w_a2_b, w_g2_b, k_k_b, k_a_b, r_k_b, gn_b_g, gn_b_b, conv_c_w, conv_c_b, lru_wa, lru_ba, lru_wx, lru_bx, lru_lambda, ln_d_g, ln_d_b, w_s_d, b_s_d, w_branch, w_out, ln1_g, ln1_b, w_up, w_down, ln2_g, ln2_b):
    batch, seq, d = x_prompt.shape
    dec_batch, dec_seq, _ = x_sample.shape
    depth = w_in.shape[0]
    db = d // 4
    d_ff = w_up.shape[2]
    n_heads = db // HEAD
    lw, la, lg = w_w2_b.shape[1], w_a2_b.shape[1], w_g2_b.shape[1]
    n_lora = lw + la + lg
    assert db % LANES == 0 and lw == HEAD and la == HEAD and lg == LANES and n_lora == WCOL
    assert w_s_d.shape[1] == db // HEAD and w_s_d.shape[2] == MLP_CHUNK
    assert dec_seq == STREAM_CHUNK
    tb_p = 256
    assert seq % tb_p == 0
    n_p, n_s = batch * seq, dec_batch * dec_seq
    n_all = n_p + n_s
    alpha = (2 * depth) ** 0.25
    blk_lora = (2 * db + 3 * db) // WCOL
    blk_gate = (9 * db + n_lora) // WCOL

    def mix_src(b):
        return b + jnp.where(b >= blk_lora, 1, 0)

    ln_dt = [F32, BF16]
    xs = _ln_rows(x_prompt.reshape(n_p, d), None, ln_in_g, ln_in_b, 1.0, 256, 0, n_p, n_all, 0, ln_dt, name="ln_in_p")
    x, xb = _ln_rows(x_sample.reshape(n_s, d), None, ln_in_g, ln_in_b, 1.0, 256, 0, n_s, n_all, n_p, ln_dt, prev=xs,
                     name="ln_in_s")

    zeros_p = dict(
        conv_a=jnp.zeros((batch, conv_a_w.shape[1] - 1, db), F32),
        shift_rkv=jnp.zeros((batch, 1, 3 * db), F32),
        shift_l=jnp.zeros((batch, 1, n_lora), F32),
        wkv=jnp.zeros((batch, n_heads // 2, LANES, LANES), F32),
        conv_c=jnp.zeros((batch, conv_c_w.shape[1] - 1, db), F32),
        lru=jnp.zeros((batch, 1, db), F32))

    blk = jnp.arange(MLP_CHUNK) // STREAM_CHUNK
    outs_p, outs_s, v_rows_s = [], [], []
    for l in range(depth):
        proj = _mm_w32(xb, w_in, l, 9 * db // WCOL, mix_src, WCOL, 2 * WCOL, 1024, F32, name="proj_mix")
        lora = _mm_w32(xb, w_in, l, 1, lambda b: b + blk_lora, WCOL, WCOL, 1024, F32, name="proj_lora")
        gates = _mm_w32(xb, w_in, l, 4 * d // WCOL, lambda b: b + blk_gate, WCOL, 2 * WCOL, 1024, BF16,
                        name="proj_gate")

        ws = jnp.where(blk[:, None] >= blk[None, :], w_s_d[l], 0.0).astype(BF16)
        half = STREAM_CHUNK
        zpad = jnp.zeros((HEAD, db), F32)
        prm = dict(
            db=db,
            conv_a_w=conv_a_w[l], conv_a_b=conv_a_b[l], ln_a_g=_row(ln_a_g[l]), ln_a_b=_row(ln_a_b[l]),
            mu_rkv=_row(mu_b[l][:3 * db]), mu_l=_row(mu_b[l][3 * db:]), w0=_row(w0_b[l]),
            ww2=_rhs3(jnp.concatenate([w_w2_b[l], zpad], axis=0)), a0=_row(a0_b[l]),
            wa2=_rhs3(jnp.concatenate([zpad, w_a2_b[l]], axis=0)), wg2=_rhs3(w_g2_b[l]),
            k_k=_row(k_k_b[l]), k_a=_row(k_a_b[l]), r_k=_row(r_k_b[l]), gn_g=_row(gn_b_g[l]), gn_b=_row(gn_b_b[l]),
            conv_c_w=conv_c_w[l], conv_c_b=_row(conv_c_b[l]),
            lru_wa=_block_diag(lru_wa[l]).astype(BF16), lru_ba=_row(lru_ba[l]),
            lru_wx=_block_diag(lru_wx[l]).astype(BF16), lru_bx=_row(lru_bx[l]),
            lru_lambda=_row(lru_lambda[l]),
            ln_d_g=_row(ln_d_g[l]), ln_d_b=_row(ln_d_b[l]),
            ws_pair=jnp.concatenate([ws[0::2], ws[1::2]], axis=2),
            ws_pair_half=jnp.concatenate([ws[0::2, :half, :half], ws[1::2, :half, :half]], axis=2),
            bias_d=jnp.repeat(b_s_d[l].T, HEAD, axis=1),
        )
        st_s = dict(
            conv_a=state_conv_a[l], shift_rkv=state_shift_b[l][:, None, :3 * db],
            shift_l=state_shift_b[l][:, None, 3 * db:], wkv=_wkv_to_pairs(state_wkv_b[l]),
            conv_c=state_conv_c[l], lru=state_lru_c[l][:, None, :])

        ys_p, new_p, _ = _mixers(proj, lora, 0, batch, seq, tb_p, zeros_p, prm, None, False, "_p")
        ys, new_s, v_rows = _mixers(proj, lora, n_p, dec_batch, dec_seq, dec_seq, st_s, prm, ys_p, True, "_s")
        merged = _merge(ys, gates, w_branch[l].astype(BF16), 512, 1024)
        o = _mm_w32(merged, w_out, l, d // WCOL, lambda b: b, WCOL, 2 * WCOL, 1024, F32, name="out_proj")
        x, xb = _ln_rows(x, o, ln1_g[l], ln1_b[l], alpha, 256, 0, n_all, n_all, 0, ln_dt, name="ln1")
        hdn = _mm_w32(xb, w_up, l, d_ff // WCOL, lambda b: b, WCOL, 2 * WCOL, 1024, BF16, act="relu2", name="mlp_up")
        o = _mm_acc(hdn, w_down[l].astype(BF16), F32, 1024, 1024, 2048, name="mlp_down")
        if l + 1 < depth:
            x, xb = _ln_rows(x, o, ln2_g[l], ln2_b[l], alpha, 256, 0, n_all, n_all, 0, ln_dt, name="ln2")
        else:
            y_prompt, = _ln_rows(x, o, ln2_g[l], ln2_b[l], alpha, 256, 0, n_p, n_p, 0, [F32], name="ln2_p")
            y_sample, = _ln_rows(x, o, ln2_g[l], ln2_b[l], alpha, 256, n_p, n_s, n_s, 0, [F32], name="ln2_s")
        outs_p.append(new_p)
        outs_s.append(new_s)
        v_rows_s.append(v_rows.reshape(dec_batch, dec_seq, db))

    def collect(outs):
        conv_a = jnp.stack([o["conv_a"] for o in outs])
        shift = jnp.stack([jnp.concatenate([o["shift_rkv"][:, 0], o["shift_l"][:, 0]], axis=1) for o in outs])
        wkv = jnp.stack([_wkv_from_pairs(o["wkv"]) for o in outs])
        conv_c = jnp.stack([o["conv_c"] for o in outs])
        lru = jnp.stack([o["lru"][:, 0] for o in outs])
        return conv_a, shift, wkv, conv_c, lru

    y_prompt = y_prompt.reshape(batch, seq, d)
    y_sample = y_sample.reshape(dec_batch, dec_seq, d)
    return (y_prompt, y_sample) + collect(outs_p) + collect(outs_s) + (jnp.stack(v_rows_s),)
```

```python
import functools
import math

import jax
import jax.numpy as jnp
from jax import lax
from jax.experimental import pallas as pl
from jax.experimental.pallas import tpu as pltpu

F32 = jnp.float32
BF16 = jnp.bfloat16

LANES = 128
HEAD = 64
WKV_CHUNK = 64
MLP_CHUNK = 128
STREAM_CHUNK = 64
WCOL = 256
MM_TM, MM_TN = 512, 1024
VMEM_LIMIT = 56 * 1024 * 1024
VMEM_LIMIT_MM = 60 * 1024 * 1024
LN_EPS = 1e-5
GN_EPS_B = 64e-5
LRU_C = 8.0


def _cparams(n_axes, vmem_limit=VMEM_LIMIT):
    return pltpu.CompilerParams(dimension_semantics=("arbitrary",) * n_axes, vmem_limit_bytes=vmem_limit)


def _dot(a, b, dims=(((1,), (0,)), ((), ()))):
    return lax.dot_general(a.astype(BF16), b.astype(BF16), dims, preferred_element_type=F32)


_NT = (((1,), (1,)), ((), ()))


def _hi_lo(x):
    hi = x.astype(BF16)
    lo = (x - hi.astype(F32)).astype(BF16)
    return hi, lo


def _lhs3(x):
    hi, lo = _hi_lo(x)
    return jnp.concatenate([hi, hi, lo], axis=1)


def _rhs3(w):
    hi, lo = _hi_lo(w.astype(F32))
    return jnp.concatenate([hi, lo, hi], axis=0)


def _gelu(x):
    return 0.5 * x * (1.0 + jnp.tanh(math.sqrt(2.0 / math.pi) * (x + 0.044715 * (x * x * x))))


def _sigmoid(x):
    return 1.0 / (1.0 + jnp.exp(-x))


def _softplus(x):
    return jnp.maximum(x, 0.0) + jnp.log1p(jnp.exp(-jnp.abs(x)))


def _layer_norm(x, g, b, eps=LN_EPS):
    mu = jnp.mean(x, axis=-1, keepdims=True)
    xc = x - mu
    var = jnp.mean(xc * xc, axis=-1, keepdims=True)
    return xc * lax.rsqrt(var + eps) * g + b


def _iota(shape, dim):
    return lax.broadcasted_iota(jnp.int32, shape, dim)


def _tile(extent, target, quantum=LANES):
    best = None
    for cand in range(quantum, min(extent, target) + 1, quantum):
        if extent % cand == 0:
            best = cand
    assert best is not None, (extent, target, quantum)
    return best


def _drop_refs(body, start, count):
    def wrapped(*refs):
        return body(*refs[:start], *refs[start + count:])
    return wrapped


def _mm_w32_kernel(a_ref, *refs, nw, cw, act):
    w_refs, o_ref, wb_ref = refs[:nw], refs[nw], refs[nw + 1]

    @pl.when(pl.program_id(1) == 0)
    def _():
        for q in range(nw):
            wb_ref[:, q * cw:(q + 1) * cw] = w_refs[q][...].astype(BF16)

    acc = jnp.dot(a_ref[...], wb_ref[...], preferred_element_type=F32)
    if act == "relu2":
        acc = jnp.square(jnp.maximum(acc, 0.0))
    o_ref[...] = acc.astype(o_ref.dtype)


def _mm_w32(a, w, layer, n_blocks, src_block, cw, tn, tm, out_dtype, act=None, name="mm"):
    m, k = a.shape
    tm = _tile(m, tm)
    nw = tn // cw
    assert n_blocks % nw == 0 and w.shape[1] == k
    w_specs = [pl.BlockSpec((None, k, cw), functools.partial(lambda j, i, q: (layer, 0, src_block(nw * j + q)), q=q))
               for q in range(nw)]
    return pl.pallas_call(
        functools.partial(_mm_w32_kernel, nw=nw, cw=cw, act=act),
        grid=(n_blocks // nw, m // tm),
        in_specs=[pl.BlockSpec((tm, k), lambda j, i: (i, 0))] + w_specs,
        out_specs=pl.BlockSpec((tm, tn), lambda j, i: (i, j)),
        out_shape=jax.ShapeDtypeStruct((m, n_blocks * cw), out_dtype),
        scratch_shapes=[pltpu.VMEM((k, tn), BF16)],
        compiler_params=_cparams(2, VMEM_LIMIT_MM), name=name,
    )(a, *([w] * nw))


def _mm_acc_kernel(a_ref, w_ref, o_ref, acc_ref):
    kk = pl.program_id(2)

    @pl.when(kk == 0)
    def _():
        acc_ref[...] = jnp.zeros_like(acc_ref)

    acc_ref[...] += jnp.dot(a_ref[...], w_ref[...], preferred_element_type=F32)

    @pl.when(kk == pl.num_programs(2) - 1)
    def _():
        o_ref[...] = acc_ref[...].astype(o_ref.dtype)


def _mm_acc(a, w, layer, out_dtype, tm, tn, tk, name="mm_acc"):
    m, k = a.shape
    n = w.shape[2]
    tm, tn, tk = _tile(m, tm), _tile(n, tn), _tile(k, tk)
    return pl.pallas_call(
        _mm_acc_kernel,
        grid=(m // tm, n // tn, k // tk),
        in_specs=[pl.BlockSpec((tm, tk), lambda i, j, q: (i, q)),
                  pl.BlockSpec((None, tk, tn), lambda i, j, q: (layer, q, j))],
        out_specs=pl.BlockSpec((tm, tn), lambda i, j, q: (i, j)),
        out_shape=jax.ShapeDtypeStruct((m, n), out_dtype),
        scratch_shapes=[pltpu.VMEM((tm, tn), F32)],
        compiler_params=_cparams(3), name=name,
    )(a, w)


def _ln_kernel(*refs, alpha, has_res, n_out):
    x_ref = refs[0]
    r_ref = refs[1] if has_res else None
    g_ref, b_ref = refs[1 + has_res], refs[2 + has_res]
    outs = refs[3 + has_res:3 + has_res + n_out]
    x = x_ref[...]
    if has_res:
        x = alpha * x + r_ref[...]
    y = _layer_norm(x, g_ref[...], b_ref[...])
    for o_ref in outs:
        o_ref[...] = y.astype(o_ref.dtype)


def _ln_rows(x, res, g, b, alpha, tm, in_row0, nrows, out_rows, out_row0, out_dtypes, prev=None, name="ln"):
    d = x.shape[1]
    tm = _tile(nrows, tm, 8)
    assert in_row0 % tm == 0 and out_row0 % tm == 0
    ib, ob = in_row0 // tm, out_row0 // tm
    row_in = pl.BlockSpec((tm, d), lambda i: (ib + i, 0))
    row_out = pl.BlockSpec((tm, d), lambda i: (ob + i, 0))
    vec = pl.BlockSpec((1, d), lambda i: (0, 0))
    has_res = res is not None
    args = [x] + ([res] if has_res else []) + [g.reshape(1, d), b.reshape(1, d)]
    specs = [row_in] * (1 + has_res) + [vec, vec]
    body = functools.partial(_ln_kernel, alpha=alpha, has_res=int(has_res), n_out=len(out_dtypes))
    aliases = {}
    if prev is not None:
        body = _drop_refs(body, len(args), len(prev))
        aliases = {len(args) + n: n for n in range(len(prev))}
        specs = specs + [pl.BlockSpec(memory_space=pl.ANY)] * len(prev)
        args = args + list(prev)
    return pl.pallas_call(
        body, grid=(nrows // tm,), in_specs=specs, out_specs=[row_out] * len(out_dtypes),
        out_shape=[jax.ShapeDtypeStruct((out_rows, d), dt) for dt in out_dtypes],
        input_output_aliases=aliases, compiler_params=_cparams(1), name=name,
    )(*args)


def _merge_kernel(ya_ref, yb_ref, yc_ref, yd_ref, g0_ref, g1_ref, g2_ref, g3_ref, wb_ref, o_ref):
    acc = None
    for n, (y_ref, g_ref) in enumerate(((ya_ref, g0_ref), (yb_ref, g1_ref), (yc_ref, g2_ref), (yd_ref, g3_ref))):
        br = jnp.dot(y_ref[...], wb_ref[n], preferred_element_type=F32)
        term = _sigmoid(g_ref[...].astype(F32)) * br
        acc = term if acc is None else acc + term
    o_ref[...] = acc.astype(o_ref.dtype)


def _merge(ys, gates, wb, layer, tm, tn):
    m, db = ys[0].shape
    d = wb.shape[3]
    tm, tn = _tile(m, tm), _tile(d, tn)
    nj = d // tn
    y_spec = pl.BlockSpec((tm, db), lambda j, i: (i, 0))
    g_specs = [pl.BlockSpec((tm, tn), functools.partial(lambda j, i, n: (i, n * nj + j), n=n)) for n in range(4)]
    return pl.pallas_call(
        _merge_kernel,
        grid=(nj, m // tm),
        in_specs=[y_spec] * 4 + g_specs + [pl.BlockSpec((None, 4, db, tn), lambda j, i: (layer, 0, 0, j))],
        out_specs=pl.BlockSpec((tm, tn), lambda j, i: (i, j)),
        out_shape=jax.ShapeDtypeStruct((m, d), BF16),
        compiler_params=_cparams(2), name="merge",
    )(*ys, gates, gates, gates, gates, wb)


def _mixer_a_kernel(a1_ref, a2_ref, hist_ref, cw_ref, cb_ref, g_ref, b_ref, y_ref, nh_ref, zext_ref, acc_ref,
                    *, tb, width, pad):
    hist = width - 1
    nchunk = zext_ref.shape[0]
    t = pl.program_id(1)

    @pl.when(t == 0)
    def _():
        h = hist_ref[0]
        for c in range(nchunk):
            zext_ref[c, 0:pad - hist, :] = jnp.zeros((pad - hist, LANES), F32)
            zext_ref[c, pad - hist:pad, :] = h[:, c * LANES:(c + 1) * LANES]

    z = a1_ref[...] * _sigmoid(a2_ref[...])
    for c in range(nchunk):
        zext_ref[c, pad:pad + tb, :] = z[:, c * LANES:(c + 1) * LANES]

    rc = 64

    def conv_chunk(c, carry):
        w = cw_ref[c]
        bias = cb_ref[c]
        for r0 in range(0, tb, rc):
            acc = jnp.broadcast_to(bias, (rc, LANES))
            for j in range(width):
                acc = acc + w[j:j + 1, :] * zext_ref[c, pl.ds(pad - hist + j + r0, rc), :]
            acc_ref[c, r0:r0 + rc, :] = acc
        return carry

    lax.fori_loop(0, nchunk, conv_chunk, 0)
    y = jnp.concatenate([acc_ref[c] for c in range(nchunk)], axis=1)
    y = _layer_norm(y, g_ref[...], b_ref[...])
    y_ref[...] = (y * _sigmoid(y)).astype(y_ref.dtype)
    nh_ref[0] = jnp.concatenate([zext_ref[c, tb + pad - hist:tb + pad, :] for c in range(nchunk)], axis=1)
    for c in range(nchunk):
        zext_ref[c, 0:pad, :] = zext_ref[c, tb:tb + pad, :]


def _mixer_c_kernel(xb_ref, gate_ref, hist_ref, h0_ref, cw_ref, cb_ref, wa_ref, ba_ref, wx_ref, bx_ref, lam_ref,
                    y_ref, nhist_ref, nh_ref, xext_ref, a_ref, u_ref, hs_ref, hcar_ref, *, tb, width, pad):
    hist = width - 1
    t = pl.program_id(1)
    db = xb_ref.shape[1]

    @pl.when(t == 0)
    def _():
        xext_ref[0:pad - hist, :] = jnp.zeros((pad - hist, db), F32)
        xext_ref[pad - hist:pad, :] = hist_ref[0]
        hcar_ref[...] = h0_ref[0]

    xext_ref[pad:pad + tb, :] = xb_ref[...]
    cw = cw_ref[...]
    xc = jnp.broadcast_to(cb_ref[...], (tb, db))
    for j in range(width):
        xc = xc + cw[j:j + 1, :] * xext_ref[pad - hist + j:pad - hist + j + tb, :]
    xcb = xc.astype(BF16)
    r = _sigmoid(jnp.dot(xcb, wa_ref[...], preferred_element_type=F32) + ba_ref[...])
    i = _sigmoid(jnp.dot(xcb, wx_ref[...], preferred_element_type=F32) + bx_ref[...])
    log_a = (LRU_C * r) * (-_softplus(-lam_ref[...]))
    a_ref[...] = jnp.exp(log_a)
    u_ref[...] = jnp.sqrt(1.0 - jnp.exp(2.0 * log_a)) * (i * xc)

    def step(k, h):
        h = a_ref[pl.ds(k, 1), :] * h + u_ref[pl.ds(k, 1), :]
        hs_ref[pl.ds(k, 1), :] = h
        return h

    h = lax.fori_loop(0, tb, step, hcar_ref[...], unroll=8)
    hcar_ref[...] = h
    y_ref[...] = (hs_ref[...] * _gelu(gate_ref[...])).astype(y_ref.dtype)
    nhist_ref[0] = xext_ref[tb + pad - hist:tb + pad, :]
    nh_ref[0] = h
    xext_ref[0:pad, :] = xext_ref[tb:tb + pad, :]


def _mixer_d_kernel(u_ref, v_ref, g_ref, b_ref, ws_ref, bias_ref, y_ref, *maybe_vrows_ref, tb, rows):
    v = _layer_norm(_gelu(v_ref[...]), g_ref[...], b_ref[...])
    for vrows_ref in maybe_vrows_ref:
        vrows_ref[...] = v
    vb = v.astype(BF16)
    npair = v.shape[1] // LANES
    first = _iota((rows, LANES), 1) < HEAD
    zero = jnp.zeros((rows, LANES), BF16)
    for r0 in range(0, tb, rows):
        for p in range(npair):
            ls = slice(p * LANES, (p + 1) * LANES)
            v2 = vb[r0:r0 + rows, ls]
            vbd = jnp.concatenate([jnp.where(first, v2, zero), jnp.where(first, zero, v2)], axis=0)
            s = jnp.dot(ws_ref[p], vbd, preferred_element_type=F32) + bias_ref[:, ls]
            y_ref[r0:r0 + rows, ls] = (_gelu(u_ref[r0:r0 + rows, ls]) * s).astype(y_ref.dtype)


def _segsum(x, ones2):
    outs = []
    for c in range(x.shape[1] // LANES):
        hi, lo = _hi_lo(x[:, c * LANES:(c + 1) * LANES])
        outs.append(jnp.dot(jnp.concatenate([hi, lo], axis=1), ones2, preferred_element_type=F32))
    return jnp.concatenate(outs, axis=1)


def _wkv_chunk(rows, at_c, rt_c, bt_c, kt_c, v_s, pc_c, st_ref, y_s, ap_ref, x_ref, ak_ref, ar_ref, rhs_ref, u_ref):
    c = WKV_CHUNK
    npair = st_ref.shape[0]
    lane = _iota((c, LANES), 1)
    first = lane < HEAD

    def bd(x):
        return jnp.concatenate([jnp.where(first, x, 0.0), jnp.where(first, 0.0, x)], axis=0)

    row = _iota((c, LANES), 0)
    col = jnp.where(first, lane, lane - HEAD)
    strict = row > col
    incl = row >= col
    eye = (_iota((LANES, LANES), 0) == _iota((LANES, LANES), 1)).astype(F32)
    same_head = (_iota((LANES, LANES), 0) < HEAD) == (_iota((LANES, LANES), 1) < HEAD)
    lanes = [slice(p * LANES, (p + 1) * LANES) for p in range(npair)]

    for p, ls in enumerate(lanes):
        g = _dot(jnp.concatenate([at_c[:, ls], rt_c[:, ls]], axis=0),
                 jnp.concatenate([bd(bt_c[:, ls]), bd(kt_c[:, ls])], axis=0), _NT)
        a_pow = bd(jnp.where(strict, g[:c, :LANES], 0.0))
        ap_ref[p] = a_pow.astype(BF16)
        x_ref[p] = eye + a_pow
        ak_ref[p] = jnp.where(strict, g[:c, LANES:], 0.0).astype(BF16)
        ar_ref[p] = jnp.concatenate([jnp.where(incl, g[c:, :LANES], 0.0),
                                     jnp.where(incl, g[c:, LANES:], 0.0)], axis=1).astype(BF16)

    for _ in range(5):
        for p in range(npair):
            a_pow = ap_ref[p]
            ap_ref[p] = jnp.dot(a_pow, a_pow, preferred_element_type=F32).astype(BF16)
        for p in range(npair):
            x = x_ref[p]
            x_ref[p] = x + jnp.dot(x.astype(BF16), ap_ref[p], preferred_element_type=F32)

    for p, ls in enumerate(lanes):
        rhs_ref[p] = _dot(at_c[:, ls], st_ref[p], _NT) + jnp.dot(ak_ref[p], bd(v_s[rows, ls]).astype(BF16),
                                                                  preferred_element_type=F32)
    for p in range(npair):
        x = x_ref[p]
        u_ref[p] = _dot(x[:c] + x[c:], bd(rhs_ref[p]))
    for p, ls in enumerate(lanes):
        u2 = u_ref[p]
        v2 = v_s[rows, ls]
        s0 = st_ref[p]
        y2 = _dot(rt_c[:, ls], s0, _NT) + jnp.dot(ar_ref[p], jnp.concatenate([bd(u2), bd(v2)], axis=0).astype(BF16),
                                                  preferred_element_type=F32)
        uvt = jnp.transpose(jnp.concatenate([u2, v2], axis=0))
        upd = _dot(uvt, jnp.concatenate([bt_c[:, ls], kt_c[:, ls]], axis=0))
        y_s[rows, ls] = y2
        st_ref[p] = jnp.where(same_head, (s0 + upd) * pc_c[:, ls], 0.0)


def _mixer_b_kernel(pr_ref, pk_ref, pv_ref, pl_ref, sh_ref, shl_ref, s0_ref,
                    mu_ref, mul_ref, w0_ref, ww2_ref, a0_ref, wa2_ref, wg2_ref, kk_ref, ka_ref, rk_ref, gg_ref, gb_ref,
                    y_ref, nsh_ref, nshl_ref, ns_ref,
                    pext_ref, lext_ref, st_ref, r_s, k_s, v_s, kk_s, b_s, ld_s, y_s,
                    at_c, rt_c, bt_c, kt_c, pc_c, ap_ref, x_ref, ak_ref, ar_ref, rhs_ref, u_ref, *, tb, pad):
    t = pl.program_id(1)
    db = pr_ref.shape[1]

    @pl.when(t == 0)
    def _():
        pext_ref[0:pad, :] = jnp.broadcast_to(sh_ref[0], (pad, 3 * db))
        lext_ref[0:pad, :] = jnp.broadcast_to(shl_ref[0], (pad, lext_ref.shape[1]))
        st_ref[...] = s0_ref[0]

    pext_ref[pad:pad + tb, 0:db] = pr_ref[...]
    pext_ref[pad:pad + tb, db:2 * db] = pk_ref[...]
    pext_ref[pad:pad + tb, 2 * db:3 * db] = pv_ref[...]
    lext_ref[pad:pad + tb, :] = pl_ref[...]
    p = pext_ref[pad:pad + tb, :]
    ps = p + mu_ref[...] * (pext_ref[pad - 1:pad - 1 + tb, :] - p)
    lo_ = lext_ref[pad:pad + tb, :]
    lo_s = lo_ + mul_ref[...] * (lext_ref[pad - 1:pad - 1 + tb, :] - lo_)
    nsh_ref[0] = pext_ref[pad + tb - 1:pad + tb, :]
    nshl_ref[0] = lext_ref[pad + tb - 1:pad + tb, :]
    pext_ref[0:pad, :] = pext_ref[tb:tb + pad, :]
    lext_ref[0:pad, :] = lext_ref[tb:tb + pad, :]

    r = ps[:, 0:db]
    k = ps[:, db:2 * db]
    v = ps[:, 2 * db:3 * db]
    wa_lo = lo_s[:, 0:LANES]
    g_lo = lo_s[:, LANES:2 * LANES]
    w = -_softplus(-(w0_ref[...] + jnp.dot(_lhs3(jnp.tanh(wa_lo)), ww2_ref[...], preferred_element_type=F32))) - 0.5
    a = _sigmoid(a0_ref[...] + jnp.dot(_lhs3(wa_lo), wa2_ref[...], preferred_element_type=F32))
    gate = jnp.dot(_lhs3(_sigmoid(g_lo)), wg2_ref[...], preferred_element_type=F32)

    lane2 = _iota((2 * LANES, LANES), 1)
    row2 = _iota((2 * LANES, LANES), 0)
    row2 = jnp.where(row2 < LANES, row2, row2 - LANES)
    ones2 = ((row2 < HEAD) == (lane2 < HEAD)).astype(BF16)

    kk = k * kk_ref[...]
    kk = kk / jnp.maximum(jnp.sqrt(_segsum(kk * kk, ones2)), 1e-12)
    k = k * (1.0 + (a - 1.0) * ka_ref[...])
    r_s[...] = r
    k_s[...] = k
    v_s[...] = v
    kk_s[...] = kk
    b_s[...] = kk * a
    ld_s[...] = -jnp.exp(w)

    c = WKV_CHUNK
    tri = (_iota((c, c), 0) >= _iota((c, c), 1)).astype(BF16)
    tri3 = jnp.concatenate([tri, tri, tri], axis=1)

    def chunk(ci, carry):
        rows = pl.ds(pl.multiple_of(ci * c, c), c)
        ld = ld_s[rows, :]
        hi = ld.astype(BF16)
        r1 = ld - hi.astype(F32)
        mid = r1.astype(BF16)
        lo = (r1 - mid.astype(F32)).astype(BF16)
        lc = jnp.dot(tri3, jnp.concatenate([hi, mid, lo], axis=0), preferred_element_type=F32)
        pcum = jnp.exp(lc)
        pinv = jnp.exp(-lc)
        at_c[...] = -(kk_s[rows, :] * jnp.exp(lc - ld))
        bt_c[...] = b_s[rows, :] * pinv
        kt_c[...] = k_s[rows, :] * pinv
        rt_c[...] = r_s[rows, :] * pcum
        pc_c[...] = pcum[c - 1:c, :]
        _wkv_chunk(rows, at_c, rt_c, bt_c, kt_c, v_s, pc_c, st_ref, y_s, ap_ref, x_ref, ak_ref, ar_ref, rhs_ref, u_ref)
        return carry

    lax.fori_loop(0, tb // c, chunk, 0)
    ns_ref[0] = st_ref[...]

    y = y_s[...]
    mu_y = _segsum(y, ones2) * (1.0 / HEAD)
    yc = y - mu_y
    var_y = _segsum(yc * yc, ones2) * (1.0 / HEAD)
    yn = yc * lax.rsqrt(var_y + GN_EPS_B) * gg_ref[...] + gb_ref[...]
    bonus = _segsum(r_s[...] * k_s[...] * rk_ref[...], ones2) * v_s[...]
    y_ref[...] = ((yn + bonus) * gate).astype(y_ref.dtype)


def _mixers(proj, lora, row0, n_streams, length, tb, st, prm, y_prev, want_v, tag):
    t_all = proj.shape[0]
    db = prm["db"]
    nb = length // tb
    off = row0 // tb
    assert row0 % tb == 0 and length % tb == 0
    grid = (n_streams, nb)

    def rows(width, col):
        return pl.BlockSpec((tb, width), lambda s, t: (off + s * nb + t, col))

    def per_stream(shape):
        nd = len(shape)
        return pl.BlockSpec((1,) + shape, lambda s, t: (s,) + (0,) * nd)

    def whole(arr):
        nd = arr.ndim
        return pl.BlockSpec(arr.shape, lambda s, t: (0,) * nd)

    y_shape = jax.ShapeDtypeStruct((t_all, db), BF16)
    y_spec = rows(db, 0)
    nchunk = db // LANES

    def call(body, n, in_specs, args, out_specs, out_shape, scratch, name):
        aliases = {}
        if y_prev is not None:
            body = _drop_refs(body, len(args), 1)
            aliases = {len(args): 0}
            in_specs = in_specs + [pl.BlockSpec(memory_space=pl.ANY)]
            args = args + [y_prev[n]]
        return pl.pallas_call(
            body, grid=grid, in_specs=in_specs, out_specs=out_specs, out_shape=out_shape, scratch_shapes=scratch,
            input_output_aliases=aliases, compiler_params=_cparams(2), name=name + tag)(*args)

    wa = prm["conv_a_w"].shape[0]
    pad_a = 32
    cw_a = prm["conv_a_w"].reshape(wa, nchunk, LANES).transpose(1, 0, 2)
    cb_a = prm["conv_a_b"].reshape(nchunk, 1, LANES)
    y_a, n_a = call(
        functools.partial(_mixer_a_kernel, tb=tb, width=wa, pad=pad_a), 0,
        [rows(db, 0), rows(db, 1), per_stream((wa - 1, db)), whole(cw_a), whole(cb_a),
         whole(prm["ln_a_g"]), whole(prm["ln_a_b"])],
        [proj, proj, st["conv_a"], cw_a, cb_a, prm["ln_a_g"], prm["ln_a_b"]],
        [y_spec, per_stream((wa - 1, db))],
        [y_shape, jax.ShapeDtypeStruct((n_streams, wa - 1, db), F32)],
        [pltpu.VMEM((nchunk, tb + pad_a, LANES), F32), pltpu.VMEM((nchunk, tb, LANES), F32)], "mixer_a")

    pad_b = 8
    nl = lora.shape[1]
    c = WKV_CHUNK
    b_w = [prm[k] for k in ("mu_rkv", "mu_l", "w0", "ww2", "a0", "wa2", "wg2", "k_k", "k_a", "r_k", "gn_g", "gn_b")]
    y_b, n_sh, n_shl, n_wkv = call(
        functools.partial(_mixer_b_kernel, tb=tb, pad=pad_b), 1,
        [rows(db, 2), rows(db, 3), rows(db, 4), pl.BlockSpec((tb, nl), lambda s, t: (off + s * nb + t, 0)),
         per_stream((1, 3 * db)), per_stream((1, nl)), per_stream((nchunk, LANES, LANES))] + [whole(w) for w in b_w],
        [proj, proj, proj, lora, st["shift_rkv"], st["shift_l"], st["wkv"]] + b_w,
        [y_spec, per_stream((1, 3 * db)), per_stream((1, nl)), per_stream((nchunk, LANES, LANES))],
        [y_shape, jax.ShapeDtypeStruct((n_streams, 1, 3 * db), F32), jax.ShapeDtypeStruct((n_streams, 1, nl), F32),
         jax.ShapeDtypeStruct((n_streams, nchunk, LANES, LANES), F32)],
        [pltpu.VMEM((tb + pad_b, 3 * db), F32), pltpu.VMEM((tb + pad_b, nl), F32),
         pltpu.VMEM((nchunk, LANES, LANES), F32)] + [pltpu.VMEM((tb, db), F32)] * 7
        + [pltpu.VMEM((c, db), F32)] * 4 + [pltpu.VMEM((1, db), F32),
                                            pltpu.VMEM((nchunk, LANES, LANES), BF16),
                                            pltpu.VMEM((nchunk, LANES, LANES), F32),
                                            pltpu.VMEM((nchunk, c, LANES), BF16),
                                            pltpu.VMEM((nchunk, c, 2 * LANES), BF16),
                                            pltpu.VMEM((nchunk, c, LANES), F32),
                                            pltpu.VMEM((nchunk, c, LANES), F32)], "mixer_b")

    wc = prm["conv_c_w"].shape[0]
    pad_c = 8
    c_w = [prm[k] for k in ("conv_c_w", "conv_c_b", "lru_wa", "lru_ba", "lru_wx", "lru_bx", "lru_lambda")]
    y_c, n_cc, n_ch = call(
        functools.partial(_mixer_c_kernel, tb=tb, width=wc, pad=pad_c), 2,
        [rows(db, 5), rows(db, 6), per_stream((wc - 1, db)), per_stream((1, db))] + [whole(w) for w in c_w],
        [proj, proj, st["conv_c"], st["lru"]] + c_w,
        [y_spec, per_stream((wc - 1, db)), per_stream((1, db))],
        [y_shape, jax.ShapeDtypeStruct((n_streams, wc - 1, db), F32), jax.ShapeDtypeStruct((n_streams, 1, db), F32)],
        [pltpu.VMEM((tb + pad_c, db), F32)] + [pltpu.VMEM((tb, db), F32)] * 3 + [pltpu.VMEM((1, db), F32)], "mixer_c")

    rows_d = min(tb, MLP_CHUNK)
    ws = prm["ws_pair"] if rows_d == MLP_CHUNK else prm["ws_pair_half"]
    bias = prm["bias_d"][:rows_d]
    d_out_specs, d_out_shape = [y_spec], [y_shape]
    if want_v:
        d_out_specs.append(pl.BlockSpec((tb, db), lambda s, t: (s * nb + t, 0)))
        d_out_shape.append(jax.ShapeDtypeStruct((n_streams * length, db), F32))
    d_out = call(
        functools.partial(_mixer_d_kernel, tb=tb, rows=rows_d), 3,
        [rows(db, 7), rows(db, 8), whole(prm["ln_d_g"]), whole(prm["ln_d_b"]), whole(ws), whole(bias)],
        [proj, proj, prm["ln_d_g"], prm["ln_d_b"], ws, bias],
        d_out_specs, d_out_shape, [], "mixer_d")
    y_d = d_out[0]
    v_rows = d_out[1] if want_v else None

    new = dict(conv_a=n_a, shift_rkv=n_sh, shift_l=n_shl, wkv=n_wkv, conv_c=n_cc, lru=n_ch)
    return (y_a, y_b, y_c, y_d), new, v_rows


def _wkv_to_pairs(s):
    top = jnp.pad(s[:, 0::2], ((0, 0), (0, 0), (0, 0), (0, HEAD)))
    bot = jnp.pad(s[:, 1::2], ((0, 0), (0, 0), (0, 0), (HEAD, 0)))
    return jnp.concatenate([top, bot], axis=2)


def _wkv_from_pairs(z):
    n, hp = z.shape[0], z.shape[1]
    s = jnp.stack([z[:, :, :HEAD, :HEAD], z[:, :, HEAD:, HEAD:]], axis=2)
    return s.reshape(n, 2 * hp, HEAD, HEAD)


def _block_diag(w):
    n, b, _ = w.shape
    tiled = jnp.tile(w.reshape(n * b, b), (1, n))
    same = (jnp.arange(n * b)[:, None] // b) == (jnp.arange(n * b)[None, :] // b)
    return jnp.where(same, tiled, 0.0)


def _row(v):
    return v.reshape(1, -1)


def kernel(x_prompt, x_sample, state_conv_a, state_shift_b, state_wkv_b, state_conv_c, state_lru_c, ln_in_g, ln_in_b, w_in, conv_a_w, conv_a_b, ln_a_g, ln_a_b, mu_b, w0_b, w_w2_b, a0_b, w_a2_b, w_g2_b, k_k_b, k_a_b, r_k_b, gn_b_g, gn_b_b, conv_c_w, conv_c_b, lru_wa, lru_ba, lru_wx, lru_bx, lru_lambda, ln_d_g, ln_d_b, w_s_d, b_s_d, w_branch, w_out, ln1_g, ln1_b, w_up, w_down, ln2_g, ln2_b):
    batch, seq, d = x_prompt.shape
    dec_batch, dec_seq, _ = x_sample.shape
    depth = w_in.shape[0]
    db = d // 4
    d_ff = w_up.shape[2]
    n_heads = db // HEAD
    lw, la, lg = w_w2_b.shape[1], w_a2_b.shape[1], w_g2_b.shape[1]
    n_lora = lw + la + lg
    assert db % LANES == 0 and lw == HEAD and la == HEAD and lg == LANES and n_lora == WCOL
    assert w_s_d.shape[1] == db // HEAD and w_s_d.shape[2] == MLP_CHUNK
    assert dec_seq == STREAM_CHUNK
    tb_p = 256
    assert seq % tb_p == 0
    n_p, n_s = batch * seq, dec_batch * dec_seq
    n_all = n_p + n_s
    alpha = (2 * depth) ** 0.25
    blk_lora = (2 * db + 3 * db) // WCOL
    blk_gate = (9 * db + n_lora) // WCOL

    def mix_src(b):
        return b + jnp.where(b >= blk_lora, 1, 0)

    ln_dt = [F32, BF16]
    xs = _ln_rows(x_prompt.reshape(n_p, d), None, ln_in_g, ln_in_b, 1.0, 256, 0, n_p, n_all, 0, ln_dt, name="ln_in_p")
    x, xb = _ln_rows(x_sample.reshape(n_s, d), None, ln_in_g, ln_in_b, 1.0, 256, 0, n_s, n_all, n_p, ln_dt, prev=xs,
                     name="ln_in_s")

    zeros_p = dict(
        conv_a=jnp.zeros((batch, conv_a_w.shape[1] - 1, db), F32),
        shift_rkv=jnp.zeros((batch, 1, 3 * db), F32),
        shift_l=jnp.zeros((batch, 1, n_lora), F32),
        wkv=jnp.zeros((batch, n_heads // 2, LANES, LANES), F32),
        conv_c=jnp.zeros((batch, conv_c_w.shape[1] - 1, db), F32),
        lru=jnp.zeros((batch, 1, db), F32))

    w_branch_b = w_branch.astype(BF16)
    w_down_b = w_down.astype(BF16)
    blk = jnp.arange(MLP_CHUNK) // STREAM_CHUNK
    outs_p, outs_s, v_rows_s = [], [], []
    for l in range(depth):
        proj = _mm_w32(xb, w_in, l, 9 * db // WCOL, mix_src, WCOL, MM_TN, MM_TM, F32, name="proj_mix")
        lora = _mm_w32(xb, w_in, l, 1, lambda b: b + blk_lora, WCOL, WCOL, 1024, F32, name="proj_lora")
        gates = _mm_w32(xb, w_in, l, 4 * d // WCOL, lambda b: b + blk_gate, WCOL, MM_TN, MM_TM, BF16,
                        name="proj_gate")

        ws = jnp.where(blk[:, None] >= blk[None, :], w_s_d[l], 0.0).astype(BF16)
        half = STREAM_CHUNK
        zpad = jnp.zeros((HEAD, db), F32)
        prm = dict(
            db=db,
            conv_a_w=conv_a_w[l], conv_a_b=conv_a_b[l], ln_a_g=_row(ln_a_g[l]), ln_a_b=_row(ln_a_b[l]),
            mu_rkv=_row(mu_b[l][:3 * db]), mu_l=_row(mu_b[l][3 * db:]), w0=_row(w0_b[l]),
            ww2=_rhs3(jnp.concatenate([w_w2_b[l], zpad], axis=0)), a0=_row(a0_b[l]),
            wa2=_rhs3(jnp.concatenate([zpad, w_a2_b[l]], axis=0)), wg2=_rhs3(w_g2_b[l]),
            k_k=_row(k_k_b[l]), k_a=_row(k_a_b[l]), r_k=_row(r_k_b[l]), gn_g=_row(gn_b_g[l]), gn_b=_row(gn_b_b[l]),
            conv_c_w=conv_c_w[l], conv_c_b=_row(conv_c_b[l]),
            lru_wa=_block_diag(lru_wa[l]).astype(BF16), lru_ba=_row(lru_ba[l]),
            lru_wx=_block_diag(lru_wx[l]).astype(BF16), lru_bx=_row(lru_bx[l]),
            lru_lambda=_row(lru_lambda[l]),
            ln_d_g=_row(ln_d_g[l]), ln_d_b=_row(ln_d_b[l]),
            ws_pair=jnp.concatenate([ws[0::2], ws[1::2]], axis=2),
            ws_pair_half=jnp.concatenate([ws[0::2, :half, :half], ws[1::2, :half, :half]], axis=2),
            bias_d=jnp.repeat(b_s_d[l].T, HEAD, axis=1),
        )
        st_s = dict(
            conv_a=state_conv_a[l], shift_rkv=state_shift_b[l][:, None, :3 * db],
            shift_l=state_shift_b[l][:, None, 3 * db:], wkv=_wkv_to_pairs(state_wkv_b[l]),
            conv_c=state_conv_c[l], lru=state_lru_c[l][:, None, :])

        ys_p, new_p, _ = _mixers(proj, lora, 0, batch, seq, tb_p, zeros_p, prm, None, False, "_p")
        ys, new_s, v_rows = _mixers(proj, lora, n_p, dec_batch, dec_seq, dec_seq, st_s, prm, ys_p, True, "_s")
        merged = _merge(ys, gates, w_branch_b, l, 512, 1024)
        o = _mm_w32(merged, w_out, l, d // WCOL, lambda b: b, WCOL, MM_TN, MM_TM, BF16, name="out_proj")
        x, xb = _ln_rows(x, o, ln1_g[l], ln1_b[l], alpha, 256, 0, n_all, n_all, 0, ln_dt, name="ln1")
        hdn = _mm_w32(xb, w_up, l, d_ff // WCOL, lambda b: b, WCOL, MM_TN, MM_TM, BF16, act="relu2", name="mlp_up")
        o = _mm_acc(hdn, w_down_b, l, BF16, 1024, 1024, 4096, name="mlp_down")
        if l + 1 < depth:
            x, xb = _ln_rows(x, o, ln2_g[l], ln2_b[l], alpha, 256, 0, n_all, n_all, 0, ln_dt, name="ln2")
        else:
            y_prompt, = _ln_rows(x, o, ln2_g[l], ln2_b[l], alpha, 256, 0, n_p, n_p, 0, [F32], name="ln2_p")
            y_sample, = _ln_rows(x, o, ln2_g[l], ln2_b[l], alpha, 256, n_p, n_s, n_s, 0, [F32], name="ln2_s")
        outs_p.append(new_p)
        outs_s.append(new_s)
        v_rows_s.append(v_rows.reshape(dec_batch, dec_seq, db))

    def collect(outs):
        conv_a = jnp.stack([o["conv_a"] for o in outs])
        shift = jnp.stack([jnp.concatenate([o["shift_rkv"][:, 0], o["shift_l"][:, 0]], axis=1) for o in outs])
        wkv = jnp.stack([_wkv_from_pairs(o["wkv"]) for o in outs])
        conv_c = jnp.stack([o["conv_c"] for o in outs])
        lru = jnp.stack([o["lru"][:, 0] for o in outs])
        return conv_a, shift, wkv, conv_c, lru

    y_prompt = y_prompt.reshape(batch, seq, d)
    y_sample = y_sample.reshape(dec_batch, dec_seq, d)
    return (y_prompt, y_sample) + collect(outs_p) + collect(outs_s) + (jnp.stack(v_rows_s),)
```

```python
import functools
import math

import jax
import jax.numpy as jnp
from jax import lax
from jax.experimental import pallas as pl
from jax.experimental.pallas import tpu as pltpu

F32 = jnp.float32
BF16 = jnp.bfloat16

LANES = 128
HEAD = 64
WKV_CHUNK = 64
MLP_CHUNK = 128
STREAM_CHUNK = 64
WCOL = 256
MM_TM, MM_TN = 1024, 1024
VMEM_LIMIT = 56 * 1024 * 1024
VMEM_LIMIT_MM = 60 * 1024 * 1024
LN_EPS = 1e-5
GN_EPS_B = 64e-5
LRU_C = 8.0


def _cparams(n_axes, vmem_limit=VMEM_LIMIT):
    return pltpu.CompilerParams(dimension_semantics=("arbitrary",) * n_axes, vmem_limit_bytes=vmem_limit)


def _dot(a, b, dims=(((1,), (0,)), ((), ()))):
    return lax.dot_general(a.astype(BF16), b.astype(BF16), dims, preferred_element_type=F32)


_NT = (((1,), (1,)), ((), ()))


def _hi_lo(x):
    hi = x.astype(BF16)
    lo = (x - hi.astype(F32)).astype(BF16)
    return hi, lo


def _lhs3(x):
    hi, lo = _hi_lo(x)
    return jnp.concatenate([hi, hi, lo], axis=1)


def _rhs3(w):
    hi, lo = _hi_lo(w.astype(F32))
    return jnp.concatenate([hi, lo, hi], axis=0)


def _gelu(x):
    return 0.5 * x * (1.0 + jnp.tanh(math.sqrt(2.0 / math.pi) * (x + 0.044715 * (x * x * x))))


def _sigmoid(x):
    return 0.5 * jnp.tanh(0.5 * x) + 0.5


def _softplus(x):
    return jnp.maximum(x, 0.0) + jnp.log1p(jnp.exp(-jnp.abs(x)))


def _layer_norm(x, g, b, eps=LN_EPS):
    mu = jnp.mean(x, axis=-1, keepdims=True)
    xc = x - mu
    var = jnp.mean(xc * xc, axis=-1, keepdims=True)
    return xc * lax.rsqrt(var + eps) * g + b


def _iota(shape, dim):
    return lax.broadcasted_iota(jnp.int32, shape, dim)


def _tile(extent, target, quantum=LANES):
    best = None
    for cand in range(quantum, min(extent, target) + 1, quantum):
        if extent % cand == 0:
            best = cand
    assert best is not None, (extent, target, quantum)
    return best


def _drop_refs(body, start, count):
    def wrapped(*refs):
        return body(*refs[:start], *refs[start + count:])
    return wrapped


def _mm_w32_kernel(a_ref, *refs, nw, cw, act):
    w_refs, o_ref, wb_ref = refs[:nw], refs[nw], refs[nw + 1]

    @pl.when(pl.program_id(1) == 0)
    def _():
        for q in range(nw):
            wb_ref[:, q * cw:(q + 1) * cw] = w_refs[q][...].astype(BF16)

    acc = jnp.dot(a_ref[...], wb_ref[...], preferred_element_type=F32)
    if act == "relu2":
        acc = jnp.square(jnp.maximum(acc, 0.0))
    o_ref[...] = acc.astype(o_ref.dtype)


def _mm_w32(a, w, layer, n_blocks, src_block, cw, tn, tm, out_dtype, act=None, name="mm"):
    m, k = a.shape
    tm = _tile(m, tm)
    nw = tn // cw
    assert n_blocks % nw == 0 and w.shape[1] == k
    w_specs = [pl.BlockSpec((None, k, cw), functools.partial(lambda j, i, q: (layer, 0, src_block(nw * j + q)), q=q),
                            pipeline_mode=pl.Buffered(1))
               for q in range(nw)]
    return pl.pallas_call(
        functools.partial(_mm_w32_kernel, nw=nw, cw=cw, act=act),
        grid=(n_blocks // nw, m // tm),
        in_specs=[pl.BlockSpec((tm, k), lambda j, i: (i, 0))] + w_specs,
        out_specs=pl.BlockSpec((tm, tn), lambda j, i: (i, j)),
        out_shape=jax.ShapeDtypeStruct((m, n_blocks * cw), out_dtype),
        scratch_shapes=[pltpu.VMEM((k, tn), BF16)],
        compiler_params=_cparams(2, VMEM_LIMIT_MM), name=name,
    )(a, *([w] * nw))


def _mm_acc_kernel(a_ref, w_ref, o_ref, acc_ref):
    kk = pl.program_id(2)

    @pl.when(kk == 0)
    def _():
        acc_ref[...] = jnp.zeros_like(acc_ref)

    acc_ref[...] += jnp.dot(a_ref[...], w_ref[...], preferred_element_type=F32)

    @pl.when(kk == pl.num_programs(2) - 1)
    def _():
        o_ref[...] = acc_ref[...].astype(o_ref.dtype)


def _mm_acc(a, w, layer, out_dtype, tm, tn, tk, name="mm_acc"):
    m, k = a.shape
    n = w.shape[2]
    tm, tn, tk = _tile(m, tm), _tile(n, tn), _tile(k, tk)
    return pl.pallas_call(
        _mm_acc_kernel,
        grid=(m // tm, n // tn, k // tk),
        in_specs=[pl.BlockSpec((tm, tk), lambda i, j, q: (i, q)),
                  pl.BlockSpec((None, tk, tn), lambda i, j, q: (layer, q, j))],
        out_specs=pl.BlockSpec((tm, tn), lambda i, j, q: (i, j)),
        out_shape=jax.ShapeDtypeStruct((m, n), out_dtype),
        scratch_shapes=[pltpu.VMEM((tm, tn), F32)],
        compiler_params=_cparams(3), name=name,
    )(a, w)


def _ln_kernel(*refs, alpha, has_res, n_out):
    x_ref = refs[0]
    r_ref = refs[1] if has_res else None
    g_ref, b_ref = refs[1 + has_res], refs[2 + has_res]
    outs = refs[3 + has_res:3 + has_res + n_out]
    x = x_ref[...]
    if has_res:
        x = alpha * x + r_ref[...]
    y = _layer_norm(x, g_ref[...], b_ref[...])
    for o_ref in outs:
        o_ref[...] = y.astype(o_ref.dtype)


def _ln_rows(x, res, g, b, alpha, tm, in_row0, nrows, out_rows, out_row0, out_dtypes, prev=None, name="ln"):
    d = x.shape[1]
    tm = _tile(nrows, tm, 8)
    assert in_row0 % tm == 0 and out_row0 % tm == 0
    ib, ob = in_row0 // tm, out_row0 // tm
    row_in = pl.BlockSpec((tm, d), lambda i: (ib + i, 0))
    row_out = pl.BlockSpec((tm, d), lambda i: (ob + i, 0))
    vec = pl.BlockSpec((1, d), lambda i: (0, 0))
    has_res = res is not None
    args = [x] + ([res] if has_res else []) + [g.reshape(1, d), b.reshape(1, d)]
    specs = [row_in] * (1 + has_res) + [vec, vec]
    body = functools.partial(_ln_kernel, alpha=alpha, has_res=int(has_res), n_out=len(out_dtypes))
    aliases = {}
    if prev is not None:
        body = _drop_refs(body, len(args), len(prev))
        aliases = {len(args) + n: n for n in range(len(prev))}
        specs = specs + [pl.BlockSpec(memory_space=pl.ANY)] * len(prev)
        args = args + list(prev)
    return pl.pallas_call(
        body, grid=(nrows // tm,), in_specs=specs, out_specs=[row_out] * len(out_dtypes),
        out_shape=[jax.ShapeDtypeStruct((out_rows, d), dt) for dt in out_dtypes],
        input_output_aliases=aliases, compiler_params=_cparams(1), name=name,
    )(*args)


def _merge_kernel(ya_ref, yb_ref, yc_ref, yd_ref, g0_ref, g1_ref, g2_ref, g3_ref, wb_ref, o_ref):
    acc = None
    for n, (y_ref, g_ref) in enumerate(((ya_ref, g0_ref), (yb_ref, g1_ref), (yc_ref, g2_ref), (yd_ref, g3_ref))):
        br = jnp.dot(y_ref[...], wb_ref[n], preferred_element_type=F32)
        term = _sigmoid(g_ref[...].astype(F32)) * br
        acc = term if acc is None else acc + term
    o_ref[...] = acc.astype(o_ref.dtype)


def _merge(ys, gates, wb, layer, tm, tn):
    m, db = ys[0].shape
    d = wb.shape[3]
    tm, tn = _tile(m, tm), _tile(d, tn)
    nj = d // tn
    y_spec = pl.BlockSpec((tm, db), lambda j, i: (i, 0))
    g_specs = [pl.BlockSpec((tm, tn), functools.partial(lambda j, i, n: (i, n * nj + j), n=n)) for n in range(4)]
    return pl.pallas_call(
        _merge_kernel,
        grid=(nj, m // tm),
        in_specs=[y_spec] * 4 + g_specs + [pl.BlockSpec((None, 4, db, tn), lambda j, i: (layer, 0, 0, j))],
        out_specs=pl.BlockSpec((tm, tn), lambda j, i: (i, j)),
        out_shape=jax.ShapeDtypeStruct((m, d), BF16),
        compiler_params=_cparams(2), name="merge",
    )(*ys, gates, gates, gates, gates, wb)


def _mixer_a_kernel(a1_ref, a2_ref, hist_ref, cw_ref, cb_ref, g_ref, b_ref, y_ref, nh_ref, zext_ref, acc_ref,
                    *, tb, width, pad):
    hist = width - 1
    nchunk = zext_ref.shape[0]
    t = pl.program_id(1)

    @pl.when(t == 0)
    def _():
        h = hist_ref[0]
        for c in range(nchunk):
            zext_ref[c, 0:pad - hist, :] = jnp.zeros((pad - hist, LANES), F32)
            zext_ref[c, pad - hist:pad, :] = h[:, c * LANES:(c + 1) * LANES]

    z = a1_ref[...] * _sigmoid(a2_ref[...])
    for c in range(nchunk):
        zext_ref[c, pad:pad + tb, :] = z[:, c * LANES:(c + 1) * LANES]

    rc = 64

    def conv_chunk(c, carry):
        w = cw_ref[c]
        bias = cb_ref[c]
        for r0 in range(0, tb, rc):
            acc = jnp.broadcast_to(bias, (rc, LANES))
            for j in range(width):
                acc = acc + w[j:j + 1, :] * zext_ref[c, pl.ds(pad - hist + j + r0, rc), :]
            acc_ref[c, r0:r0 + rc, :] = acc
        return carry

    lax.fori_loop(0, nchunk, conv_chunk, 0)
    y = jnp.concatenate([acc_ref[c] for c in range(nchunk)], axis=1)
    y = _layer_norm(y, g_ref[...], b_ref[...])
    y_ref[...] = (y * _sigmoid(y)).astype(y_ref.dtype)
    nh_ref[0] = jnp.concatenate([zext_ref[c, tb + pad - hist:tb + pad, :] for c in range(nchunk)], axis=1)
    for c in range(nchunk):
        zext_ref[c, 0:pad, :] = zext_ref[c, tb:tb + pad, :]


def _mixer_c_kernel(xb_ref, gate_ref, hist_ref, h0_ref, cw_ref, cb_ref, wa_ref, ba_ref, wx_ref, bx_ref, lam_ref,
                    y_ref, nhist_ref, nh_ref, xext_ref, a_ref, u_ref, hs_ref, hcar_ref, *, tb, width, pad):
    hist = width - 1
    t = pl.program_id(1)
    db = xb_ref.shape[1]

    @pl.when(t == 0)
    def _():
        xext_ref[0:pad - hist, :] = jnp.zeros((pad - hist, db), F32)
        xext_ref[pad - hist:pad, :] = hist_ref[0]
        hcar_ref[...] = h0_ref[0]

    xext_ref[pad:pad + tb, :] = xb_ref[...]
    cw = cw_ref[...]
    xc = jnp.broadcast_to(cb_ref[...], (tb, db))
    for j in range(width):
        xc = xc + cw[j:j + 1, :] * xext_ref[pad - hist + j:pad - hist + j + tb, :]
    xcb = xc.astype(BF16)
    r = _sigmoid(jnp.dot(xcb, wa_ref[...], preferred_element_type=F32) + ba_ref[...])
    i = _sigmoid(jnp.dot(xcb, wx_ref[...], preferred_element_type=F32) + bx_ref[...])
    log_a = (LRU_C * r) * (-_softplus(-lam_ref[...]))
    a = jnp.exp(log_a)
    a_ref[...] = a
    u_ref[...] = jnp.sqrt(-jnp.tanh(log_a) * (a * a + 1.0)) * (i * xc)

    def step(k, h):
        h = a_ref[pl.ds(k, 1), :] * h + u_ref[pl.ds(k, 1), :]
        hs_ref[pl.ds(k, 1), :] = h
        return h

    h = lax.fori_loop(0, tb, step, hcar_ref[...], unroll=8)
    hcar_ref[...] = h
    y_ref[...] = (hs_ref[...] * _gelu(gate_ref[...])).astype(y_ref.dtype)
    nhist_ref[0] = xext_ref[tb + pad - hist:tb + pad, :]
    nh_ref[0] = h
    xext_ref[0:pad, :] = xext_ref[tb:tb + pad, :]


def _mixer_d_kernel(u_ref, v_ref, g_ref, b_ref, ws_ref, bias_ref, y_ref, *maybe_vrows_ref, tb, rows):
    v = _layer_norm(_gelu(v_ref[...]), g_ref[...], b_ref[...])
    for vrows_ref in maybe_vrows_ref:
        vrows_ref[...] = v
    vb = v.astype(BF16)
    npair = v.shape[1] // LANES
    first = _iota((rows, LANES), 1) < HEAD
    zero = jnp.zeros((rows, LANES), BF16)
    for r0 in range(0, tb, rows):
        for p in range(npair):
            ls = slice(p * LANES, (p + 1) * LANES)
            v2 = vb[r0:r0 + rows, ls]
            vbd = jnp.concatenate([jnp.where(first, v2, zero), jnp.where(first, zero, v2)], axis=0)
            s = jnp.dot(ws_ref[p], vbd, preferred_element_type=F32) + bias_ref[:, ls]
            y_ref[r0:r0 + rows, ls] = (_gelu(u_ref[r0:r0 + rows, ls]) * s).astype(y_ref.dtype)


def _segsum(x, ones_bd):
    outs = [jnp.dot(x[:, c * LANES:(c + 1) * LANES].astype(BF16), ones_bd, preferred_element_type=F32)
            for c in range(x.shape[1] // LANES)]
    return jnp.concatenate(outs, axis=1)


def _wkv_chunk(rows, at_c, rt_c, bt_c, kt_c, v_s, pc_c, st_ref, y_s, ap_ref, x_ref, ak_ref, ar_ref, rhs_ref, u_ref):
    c = WKV_CHUNK
    npair = st_ref.shape[0]
    lane = _iota((c, LANES), 1)
    first = lane < HEAD

    def bd(x):
        return jnp.concatenate([jnp.where(first, x, 0.0), jnp.where(first, 0.0, x)], axis=0)

    row = _iota((c, LANES), 0)
    col = jnp.where(first, lane, lane - HEAD)
    strict = row > col
    incl = row >= col
    eye = (_iota((LANES, LANES), 0) == _iota((LANES, LANES), 1)).astype(F32)
    same_head = (_iota((LANES, LANES), 0) < HEAD) == (_iota((LANES, LANES), 1) < HEAD)
    lanes = [slice(p * LANES, (p + 1) * LANES) for p in range(npair)]

    for p, ls in enumerate(lanes):
        g = _dot(jnp.concatenate([at_c[:, ls], rt_c[:, ls]], axis=0),
                 jnp.concatenate([bd(bt_c[:, ls]), bd(kt_c[:, ls])], axis=0), _NT)
        a_pow = bd(jnp.where(strict, g[:c, :LANES], 0.0))
        ap_ref[p] = a_pow.astype(BF16)
        x_ref[p] = eye + a_pow
        ak_ref[p] = jnp.where(strict, g[:c, LANES:], 0.0).astype(BF16)
        ar_ref[p] = jnp.concatenate([jnp.where(incl, g[c:, :LANES], 0.0),
                                     jnp.where(incl, g[c:, LANES:], 0.0)], axis=1).astype(BF16)

    for p in range(npair):
        a_pow = ap_ref[p]
        ap_ref[p] = jnp.dot(a_pow, a_pow, preferred_element_type=F32).astype(BF16)
    for step in range(5):
        for p in range(npair):
            x = x_ref[p]
            a_pow = ap_ref[p]
            if step < 4:
                both = jnp.dot(jnp.concatenate([x.astype(BF16), a_pow], axis=0), a_pow, preferred_element_type=F32)
                x_ref[p] = x + both[:LANES]
                ap_ref[p] = both[LANES:].astype(BF16)
            else:
                x_ref[p] = x + jnp.dot(x.astype(BF16), a_pow, preferred_element_type=F32)

    for p, ls in enumerate(lanes):
        rhs_ref[p] = _dot(at_c[:, ls], st_ref[p], _NT) + jnp.dot(ak_ref[p], bd(v_s[rows, ls]).astype(BF16),
                                                                  preferred_element_type=F32)
    for p in range(npair):
        x = x_ref[p]
        u_ref[p] = _dot(x[:c] + x[c:], bd(rhs_ref[p]))
    for p, ls in enumerate(lanes):
        u2 = u_ref[p]
        v2 = v_s[rows, ls]
        s0 = st_ref[p]
        y2 = _dot(rt_c[:, ls], s0, _NT) + jnp.dot(ar_ref[p], jnp.concatenate([bd(u2), bd(v2)], axis=0).astype(BF16),
                                                  preferred_element_type=F32)
        uvt = jnp.transpose(jnp.concatenate([u2, v2], axis=0))
        upd = _dot(uvt, jnp.concatenate([bt_c[:, ls], kt_c[:, ls]], axis=0))
        y_s[rows, ls] = y2
        st_ref[p] = jnp.where(same_head, (s0 + upd) * pc_c[:, ls], 0.0)


def _mixer_b_kernel(pr_ref, pk_ref, pv_ref, pl_ref, sh_ref, shl_ref, s0_ref,
                    mu_ref, mul_ref, w0_ref, ww2_ref, a0_ref, wa2_ref, wg2_ref, kk_ref, ka_ref, rk_ref, gg_ref, gb_ref,
                    y_ref, nsh_ref, nshl_ref, ns_ref,
                    pext_ref, lext_ref, st_ref, r_s, k_s, v_s, kk_s, b_s, ld_s, y_s,
                    at_c, rt_c, bt_c, kt_c, pc_c, ap_ref, x_ref, ak_ref, ar_ref, rhs_ref, u_ref, *, tb, pad):
    t = pl.program_id(1)
    db = pr_ref.shape[1]

    @pl.when(t == 0)
    def _():
        pext_ref[0:pad, :] = jnp.broadcast_to(sh_ref[0], (pad, 3 * db))
        lext_ref[0:pad, :] = jnp.broadcast_to(shl_ref[0], (pad, lext_ref.shape[1]))
        st_ref[...] = s0_ref[0]

    pext_ref[pad:pad + tb, 0:db] = pr_ref[...]
    pext_ref[pad:pad + tb, db:2 * db] = pk_ref[...]
    pext_ref[pad:pad + tb, 2 * db:3 * db] = pv_ref[...]
    lext_ref[pad:pad + tb, :] = pl_ref[...]
    p = pext_ref[pad:pad + tb, :]
    ps = p + mu_ref[...] * (pext_ref[pad - 1:pad - 1 + tb, :] - p)
    lo_ = lext_ref[pad:pad + tb, :]
    lo_s = lo_ + mul_ref[...] * (lext_ref[pad - 1:pad - 1 + tb, :] - lo_)
    nsh_ref[0] = pext_ref[pad + tb - 1:pad + tb, :]
    nshl_ref[0] = lext_ref[pad + tb - 1:pad + tb, :]
    pext_ref[0:pad, :] = pext_ref[tb:tb + pad, :]
    lext_ref[0:pad, :] = lext_ref[tb:tb + pad, :]

    r = ps[:, 0:db]
    k = ps[:, db:2 * db]
    v = ps[:, 2 * db:3 * db]
    wa_lo = lo_s[:, 0:LANES]
    g_lo = lo_s[:, LANES:2 * LANES]
    w = -_softplus(-(w0_ref[...] + jnp.dot(_lhs3(jnp.tanh(wa_lo)), ww2_ref[...], preferred_element_type=F32))) - 0.5
    a = _sigmoid(a0_ref[...] + jnp.dot(_lhs3(wa_lo), wa2_ref[...], preferred_element_type=F32))
    gate = jnp.dot(_lhs3(_sigmoid(g_lo)), wg2_ref[...], preferred_element_type=F32)

    ones2 = ((_iota((LANES, LANES), 0) < HEAD) == (_iota((LANES, LANES), 1) < HEAD)).astype(BF16)

    kk = k * kk_ref[...]
    kk = kk * lax.rsqrt(jnp.maximum(_segsum(kk * kk, ones2), 1e-24))
    k = k * (1.0 + (a - 1.0) * ka_ref[...])
    r_s[...] = r
    k_s[...] = k
    v_s[...] = v
    kk_s[...] = kk
    b_s[...] = kk * a
    ld_s[...] = -jnp.exp(w)

    c = WKV_CHUNK
    tri = (_iota((c, c), 0) >= _iota((c, c), 1)).astype(BF16)
    tri3 = jnp.concatenate([tri, tri, tri], axis=1)

    def chunk(ci, carry):
        rows = pl.ds(pl.multiple_of(ci * c, c), c)
        ld = ld_s[rows, :]
        hi = ld.astype(BF16)
        r1 = ld - hi.astype(F32)
        mid = r1.astype(BF16)
        lo = (r1 - mid.astype(F32)).astype(BF16)
        lc = jnp.dot(tri3, jnp.concatenate([hi, mid, lo], axis=0), preferred_element_type=F32)
        pcum = jnp.exp(lc)
        pinv = jnp.exp(-lc)
        at_c[...] = -(kk_s[rows, :] * jnp.exp(lc - ld))
        bt_c[...] = b_s[rows, :] * pinv
        kt_c[...] = k_s[rows, :] * pinv
        rt_c[...] = r_s[rows, :] * pcum
        pc_c[...] = pcum[c - 1:c, :]
        _wkv_chunk(rows, at_c, rt_c, bt_c, kt_c, v_s, pc_c, st_ref, y_s, ap_ref, x_ref, ak_ref, ar_ref, rhs_ref, u_ref)
        return carry

    lax.fori_loop(0, tb // c, chunk, 0)
    ns_ref[0] = st_ref[...]

    y = y_s[...]
    mu_y = _segsum(y, ones2) * (1.0 / HEAD)
    yc = y - mu_y
    var_y = _segsum(yc * yc, ones2) * (1.0 / HEAD)
    yn = yc * lax.rsqrt(var_y + GN_EPS_B) * gg_ref[...] + gb_ref[...]
    bonus = _segsum(r_s[...] * k_s[...] * rk_ref[...], ones2) * v_s[...]
    y_ref[...] = ((yn + bonus) * gate).astype(y_ref.dtype)


def _mixers(proj, lora, row0, n_streams, length, tb, st, prm, y_prev, want_v, tag):
    t_all = proj.shape[0]
    db = prm["db"]
    nb = length // tb
    off = row0 // tb
    assert row0 % tb == 0 and length % tb == 0
    grid = (n_streams, nb)

    def rows(width, col):
        return pl.BlockSpec((tb, width), lambda s, t: (off + s * nb + t, col))

    def per_stream(shape):
        nd = len(shape)
        return pl.BlockSpec((1,) + shape, lambda s, t: (s,) + (0,) * nd)

    def whole(arr):
        nd = arr.ndim
        return pl.BlockSpec(arr.shape, lambda s, t: (0,) * nd)

    y_shape = jax.ShapeDtypeStruct((t_all, db), BF16)
    y_spec = rows(db, 0)
    nchunk = db // LANES

    def call(body, n, in_specs, args, out_specs, out_shape, scratch, name):
        aliases = {}
        if y_prev is not None:
            body = _drop_refs(body, len(args), 1)
            aliases = {len(args): 0}
            in_specs = in_specs + [pl.BlockSpec(memory_space=pl.ANY)]
            args = args + [y_prev[n]]
        return pl.pallas_call(
            body, grid=grid, in_specs=in_specs, out_specs=out_specs, out_shape=out_shape, scratch_shapes=scratch,
            input_output_aliases=aliases, compiler_params=_cparams(2), name=name + tag)(*args)

    wa = prm["conv_a_w"].shape[0]
    pad_a = 32
    cw_a = prm["conv_a_w"].reshape(wa, nchunk, LANES).transpose(1, 0, 2)
    cb_a = prm["conv_a_b"].reshape(nchunk, 1, LANES)
    y_a, n_a = call(
        functools.partial(_mixer_a_kernel, tb=tb, width=wa, pad=pad_a), 0,
        [rows(db, 0), rows(db, 1), per_stream((wa - 1, db)), whole(cw_a), whole(cb_a),
         whole(prm["ln_a_g"]), whole(prm["ln_a_b"])],
        [proj, proj, st["conv_a"], cw_a, cb_a, prm["ln_a_g"], prm["ln_a_b"]],
        [y_spec, per_stream((wa - 1, db))],
        [y_shape, jax.ShapeDtypeStruct((n_streams, wa - 1, db), F32)],
        [pltpu.VMEM((nchunk, tb + pad_a, LANES), F32), pltpu.VMEM((nchunk, tb, LANES), F32)], "mixer_a")

    pad_b = 8
    nl = lora.shape[1]
    c = WKV_CHUNK
    b_w = [prm[k] for k in ("mu_rkv", "mu_l", "w0", "ww2", "a0", "wa2", "wg2", "k_k", "k_a", "r_k", "gn_g", "gn_b")]
    y_b, n_sh, n_shl, n_wkv = call(
        functools.partial(_mixer_b_kernel, tb=tb, pad=pad_b), 1,
        [rows(db, 2), rows(db, 3), rows(db, 4), pl.BlockSpec((tb, nl), lambda s, t: (off + s * nb + t, 0)),
         per_stream((1, 3 * db)), per_stream((1, nl)), per_stream((nchunk, LANES, LANES))] + [whole(w) for w in b_w],
        [proj, proj, proj, lora, st["shift_rkv"], st["shift_l"], st["wkv"]] + b_w,
        [y_spec, per_stream((1, 3 * db)), per_stream((1, nl)), per_stream((nchunk, LANES, LANES))],
        [y_shape, jax.ShapeDtypeStruct((n_streams, 1, 3 * db), F32), jax.ShapeDtypeStruct((n_streams, 1, nl), F32),
         jax.ShapeDtypeStruct((n_streams, nchunk, LANES, LANES), F32)],
        [pltpu.VMEM((tb + pad_b, 3 * db), F32), pltpu.VMEM((tb + pad_b, nl), F32),
         pltpu.VMEM((nchunk, LANES, LANES), F32)] + [pltpu.VMEM((tb, db), F32)] * 7
        + [pltpu.VMEM((c, db), F32)] * 4 + [pltpu.VMEM((1, db), F32),
                                            pltpu.VMEM((nchunk, LANES, LANES), BF16),
                                            pltpu.VMEM((nchunk, LANES, LANES), F32),
                                            pltpu.VMEM((nchunk, c, LANES), BF16),
                                            pltpu.VMEM((nchunk, c, 2 * LANES), BF16),
                                            pltpu.VMEM((nchunk, c, LANES), F32),
                                            pltpu.VMEM((nchunk, c, LANES), F32)], "mixer_b")

    wc = prm["conv_c_w"].shape[0]
    pad_c = 8
    c_w = [prm[k] for k in ("conv_c_w", "conv_c_b", "lru_wa", "lru_ba", "lru_wx", "lru_bx", "lru_lambda")]
    y_c, n_cc, n_ch = call(
        functools.partial(_mixer_c_kernel, tb=tb, width=wc, pad=pad_c), 2,
        [rows(db, 5), rows(db, 6), per_stream((wc - 1, db)), per_stream((1, db))] + [whole(w) for w in c_w],
        [proj, proj, st["conv_c"], st["lru"]] + c_w,
        [y_spec, per_stream((wc - 1, db)), per_stream((1, db))],
        [y_shape, jax.ShapeDtypeStruct((n_streams, wc - 1, db), F32), jax.ShapeDtypeStruct((n_streams, 1, db), F32)],
        [pltpu.VMEM((tb + pad_c, db), F32)] + [pltpu.VMEM((tb, db), F32)] * 3 + [pltpu.VMEM((1, db), F32)], "mixer_c")

    rows_d = min(tb, MLP_CHUNK)
    ws = prm["ws_pair"] if rows_d == MLP_CHUNK else prm["ws_pair_half"]
    bias = prm["bias_d"][:rows_d]
    d_out_specs, d_out_shape = [y_spec], [y_shape]
    if want_v:
        d_out_specs.append(pl.BlockSpec((tb, db), lambda s, t: (s * nb + t, 0)))
        d_out_shape.append(jax.ShapeDtypeStruct((n_streams * length, db), F32))
    d_out = call(
        functools.partial(_mixer_d_kernel, tb=tb, rows=rows_d), 3,
        [rows(db, 7), rows(db, 8), whole(prm["ln_d_g"]), whole(prm["ln_d_b"]), whole(ws), whole(bias)],
        [proj, proj, prm["ln_d_g"], prm["ln_d_b"], ws, bias],
        d_out_specs, d_out_shape, [], "mixer_d")
    y_d = d_out[0]
    v_rows = d_out[1] if want_v else None

    new = dict(conv_a=n_a, shift_rkv=n_sh, shift_l=n_shl, wkv=n_wkv, conv_c=n_cc, lru=n_ch)
    return (y_a, y_b, y_c, y_d), new, v_rows


def _wkv_to_pairs(s):
    top = jnp.pad(s[:, 0::2], ((0, 0), (0, 0), (0, 0), (0, HEAD)))
    bot = jnp.pad(s[:, 1::2], ((0, 0), (0, 0), (0, 0), (HEAD, 0)))
    return jnp.concatenate([top, bot], axis=2)


def _wkv_from_pairs(z):
    n, hp = z.shape[0], z.shape[1]
    s = jnp.stack([z[:, :, :HEAD, :HEAD], z[:, :, HEAD:, HEAD:]], axis=2)
    return s.reshape(n, 2 * hp, HEAD, HEAD)


def _block_diag(w):
    n, b, _ = w.shape
    tiled = jnp.tile(w.reshape(n * b, b), (1, n))
    same = (jnp.arange(n * b)[:, None] // b) == (jnp.arange(n * b)[None, :] // b)
    return jnp.where(same, tiled, 0.0)


def _row(v):
    return v.reshape(1, -1)


def kernel(x_prompt, x_sample, state_conv_a, state_shift_b, state_wkv_b, state_conv_c, state_lru_c, ln_in_g, ln_in_b, w_in, conv_a_w, conv_a_b, ln_a_g, ln_a_b, mu_b, w0_b, w_w2_b, a0_b, w_a2_b, w_g2_b, k_k_b, k_a_b, r_k_b, gn_b_g, gn_b_b, conv_c_w, conv_c_b, lru_wa, lru_ba, lru_wx, lru_bx, lru_lambda, ln_d_g, ln_d_b, w_s_d, b_s_d, w_branch, w_out, ln1_g, ln1_b, w_up, w_down, ln2_g, ln2_b):
    batch, seq, d = x_prompt.shape
    dec_batch, dec_seq, _ = x_sample.shape
    depth = w_in.shape[0]
    db = d // 4
    d_ff = w_up.shape[2]
    n_heads = db // HEAD
    lw, la, lg = w_w2_b.shape[1], w_a2_b.shape[1], w_g2_b.shape[1]
    n_lora = lw + la + lg
    assert db % LANES == 0 and lw == HEAD and la == HEAD and lg == LANES and n_lora == WCOL
    assert w_s_d.shape[1] == db // HEAD and w_s_d.shape[2] == MLP_CHUNK
    assert dec_seq == STREAM_CHUNK
    tb_p = 256
    assert seq % tb_p == 0
    n_p, n_s = batch * seq, dec_batch * dec_seq
    n_all = n_p + n_s
    alpha = (2 * depth) ** 0.25
    blk_lora = (2 * db + 3 * db) // WCOL
    blk_gate = (9 * db + n_lora) // WCOL

    def mix_src(b):
        return b + jnp.where(b >= blk_lora, 1, 0)

    ln_dt = [F32, BF16]
    xs = _ln_rows(x_prompt.reshape(n_p, d), None, ln_in_g, ln_in_b, 1.0, 256, 0, n_p, n_all, 0, ln_dt, name="ln_in_p")
    x, xb = _ln_rows(x_sample.reshape(n_s, d), None, ln_in_g, ln_in_b, 1.0, 256, 0, n_s, n_all, n_p, ln_dt, prev=xs,
                     name="ln_in_s")

    zeros_p = dict(
        conv_a=jnp.zeros((batch, conv_a_w.shape[1] - 1, db), F32),
        shift_rkv=jnp.zeros((batch, 1, 3 * db), F32),
        shift_l=jnp.zeros((batch, 1, n_lora), F32),
        wkv=jnp.zeros((batch, n_heads // 2, LANES, LANES), F32),
        conv_c=jnp.zeros((batch, conv_c_w.shape[1] - 1, db), F32),
        lru=jnp.zeros((batch, 1, db), F32))

    w_branch_b = w_branch.astype(BF16)
    w_down_b = w_down.astype(BF16)
    blk = jnp.arange(MLP_CHUNK) // STREAM_CHUNK
    outs_p, outs_s, v_rows_s = [], [], []
    for l in range(depth):
        proj = _mm_w32(xb, w_in, l, 9 * db // WCOL, mix_src, WCOL, MM_TN, MM_TM, F32, name="proj_mix")
        lora = _mm_w32(xb, w_in, l, 1, lambda b: b + blk_lora, WCOL, WCOL, 1024, F32, name="proj_lora")
        gates = _mm_w32(xb, w_in, l, 4 * d // WCOL, lambda b: b + blk_gate, WCOL, MM_TN, MM_TM, BF16,
                        name="proj_gate")

        ws = jnp.where(blk[:, None] >= blk[None, :], w_s_d[l], 0.0).astype(BF16)
        half = STREAM_CHUNK
        zpad = jnp.zeros((HEAD, db), F32)
        prm = dict(
            db=db,
            conv_a_w=conv_a_w[l], conv_a_b=conv_a_b[l], ln_a_g=_row(ln_a_g[l]), ln_a_b=_row(ln_a_b[l]),
            mu_rkv=_row(mu_b[l][:3 * db]), mu_l=_row(mu_b[l][3 * db:]), w0=_row(w0_b[l]),
            ww2=_rhs3(jnp.concatenate([w_w2_b[l], zpad], axis=0)), a0=_row(a0_b[l]),
            wa2=_rhs3(jnp.concatenate([zpad, w_a2_b[l]], axis=0)), wg2=_rhs3(w_g2_b[l]),
            k_k=_row(k_k_b[l]), k_a=_row(k_a_b[l]), r_k=_row(r_k_b[l]), gn_g=_row(gn_b_g[l]), gn_b=_row(gn_b_b[l]),
            conv_c_w=conv_c_w[l], conv_c_b=_row(conv_c_b[l]),
            lru_wa=_block_diag(lru_wa[l]).astype(BF16), lru_ba=_row(lru_ba[l]),
            lru_wx=_block_diag(lru_wx[l]).astype(BF16), lru_bx=_row(lru_bx[l]),
            lru_lambda=_row(lru_lambda[l]),
            ln_d_g=_row(ln_d_g[l]), ln_d_b=_row(ln_d_b[l]),
            ws_pair=jnp.concatenate([ws[0::2], ws[1::2]], axis=2),
            ws_pair_half=jnp.concatenate([ws[0::2, :half, :half], ws[1::2, :half, :half]], axis=2),
            bias_d=jnp.repeat(b_s_d[l].T, HEAD, axis=1),
        )
        st_s = dict(
            conv_a=state_conv_a[l], shift_rkv=state_shift_b[l][:, None, :3 * db],
            shift_l=state_shift_b[l][:, None, 3 * db:], wkv=_wkv_to_pairs(state_wkv_b[l]),
            conv_c=state_conv_c[l], lru=state_lru_c[l][:, None, :])

        ys_p, new_p, _ = _mixers(proj, lora, 0, batch, seq, tb_p, zeros_p, prm, None, False, "_p")
        ys, new_s, v_rows = _mixers(proj, lora, n_p, dec_batch, dec_seq, dec_seq, st_s, prm, ys_p, True, "_s")
        merged = _merge(ys, gates, w_branch_b, l, 512, 1024)
        o = _mm_w32(merged, w_out, l, d // WCOL, lambda b: b, WCOL, MM_TN, MM_TM, BF16, name="out_proj")
        x, xb = _ln_rows(x, o, ln1_g[l], ln1_b[l], alpha, 256, 0, n_all, n_all, 0, ln_dt, name="ln1")
        hdn = _mm_w32(xb, w_up, l, d_ff // WCOL, lambda b: b, WCOL, MM_TN, MM_TM, BF16, act="relu2", name="mlp_up")
        o = _mm_acc(hdn, w_down_b, l, BF16, 1024, 1024, 4096, name="mlp_down")
        if l + 1 < depth:
            x, xb = _ln_rows(x, o, ln2_g[l], ln2_b[l], alpha, 256, 0, n_all, n_all, 0, ln_dt, name="ln2")
        else:
            y_prompt, = _ln_rows(x, o, ln2_g[l], ln2_b[l], alpha, 256, 0, n_p, n_p, 0, [F32], name="ln2_p")
            y_sample, = _ln_rows(x, o, ln2_g[l], ln2_b[l], alpha, 256, n_p, n_s, n_s, 0, [F32], name="ln2_s")
        outs_p.append(new_p)
        outs_s.append(new_s)
        v_rows_s.append(v_rows.reshape(dec_batch, dec_seq, db))

    def collect(outs):
        conv_a = jnp.stack([o["conv_a"] for o in outs])
        shift = jnp.stack([jnp.concatenate([o["shift_rkv"][:, 0], o["shift_l"][:, 0]], axis=1) for o in outs])
        wkv = jnp.stack([_wkv_from_pairs(o["wkv"]) for o in outs])
        conv_c = jnp.stack([o["conv_c"] for o in outs])
        lru = jnp.stack([o["lru"][:, 0] for o in outs])
        return conv_a, shift, wkv, conv_c, lru

    y_prompt = y_prompt.reshape(batch, seq, d)
    y_sample = y_sample.reshape(dec_batch, dec_seq, d)
    return (y_prompt, y_sample) + collect(outs_p) + collect(outs_s) + (jnp.stack(v_rows_s),)
```

```python
import functools
import math

import jax
import jax.numpy as jnp
from jax import lax
from jax.experimental import pallas as pl
from jax.experimental.pallas import tpu as pltpu

F32 = jnp.float32
BF16 = jnp.bfloat16

LANES = 128
HEAD = 64
WKV_CHUNK = 64
MLP_CHUNK = 128
STREAM_CHUNK = 64
WCOL = 256
MM_TM, MM_TN = 1024, 1024
VMEM_LIMIT = 56 * 1024 * 1024
VMEM_LIMIT_MM = 60 * 1024 * 1024
LN_EPS = 1e-5
GN_EPS_B = 64e-5
LRU_C = 8.0


def _cparams(n_axes, vmem_limit=VMEM_LIMIT):
    return pltpu.CompilerParams(dimension_semantics=("arbitrary",) * n_axes, vmem_limit_bytes=vmem_limit)


def _dot(a, b, dims=(((1,), (0,)), ((), ()))):
    return lax.dot_general(a.astype(BF16), b.astype(BF16), dims, preferred_element_type=F32)


_NT = (((1,), (1,)), ((), ()))


def _hi_lo(x):
    hi = x.astype(BF16)
    lo = (x - hi.astype(F32)).astype(BF16)
    return hi, lo


def _lhs3(x):
    hi, lo = _hi_lo(x)
    return jnp.concatenate([hi, hi, lo], axis=1)


def _rhs3(w):
    hi, lo = _hi_lo(w.astype(F32))
    return jnp.concatenate([hi, lo, hi], axis=0)


def _gelu(x):
    return 0.5 * x * (1.0 + jnp.tanh(math.sqrt(2.0 / math.pi) * (x + 0.044715 * (x * x * x))))


def _sigmoid(x):
    return 0.5 * jnp.tanh(0.5 * x) + 0.5


def _softplus(x):
    return jnp.maximum(x, 0.0) + jnp.log1p(jnp.exp(-jnp.abs(x)))


def _layer_norm(x, g, b, eps=LN_EPS):
    mu = jnp.mean(x, axis=-1, keepdims=True)
    xc = x - mu
    var = jnp.mean(xc * xc, axis=-1, keepdims=True)
    return xc * lax.rsqrt(var + eps) * g + b


def _iota(shape, dim):
    return lax.broadcasted_iota(jnp.int32, shape, dim)


def _tile(extent, target, quantum=LANES):
    best = None
    for cand in range(quantum, min(extent, target) + 1, quantum):
        if extent % cand == 0:
            best = cand
    assert best is not None, (extent, target, quantum)
    return best


def _mm_w32_kernel(a_ref, *refs, nw, cw, act, has_side):
    w_refs, refs = refs[:nw], refs[nw:]
    if has_side:
        side_ref, o_ref, side_o_ref, wb_ref = refs
        side_o_ref[...] = side_ref[...].astype(BF16)
    else:
        o_ref, wb_ref = refs

    @pl.when(pl.program_id(1) == 0)
    def _():
        for q in range(nw):
            wb_ref[:, q * cw:(q + 1) * cw] = w_refs[q][...].astype(BF16)

    acc = jnp.dot(a_ref[...], wb_ref[...], preferred_element_type=F32)
    if act == "relu2":
        acc = jnp.square(jnp.maximum(acc, 0.0))
    o_ref[...] = acc.astype(o_ref.dtype)


def _mm_w32(a, w, layer, n_blocks, src_block, cw, tn, tm, out_dtype, act=None, side=None, name="mm"):
    m, k = a.shape
    tm = _tile(m, tm)
    nw = tn // cw
    assert n_blocks % nw == 0 and w.shape[1] == k
    nj, ni = n_blocks // nw, m // tm
    w_specs = [pl.BlockSpec((None, k, cw), functools.partial(lambda j, i, q: (layer, 0, src_block(nw * j + q)), q=q),
                            pipeline_mode=pl.Buffered(1))
               for q in range(nw)]
    in_specs = [pl.BlockSpec((tm, k), lambda j, i: (i, 0))] + w_specs
    out_specs = [pl.BlockSpec((tm, tn), lambda j, i: (i, j))]
    out_shape = [jax.ShapeDtypeStruct((m, n_blocks * cw), out_dtype)]
    args = [a] + [w] * nw
    if side is not None:
        _, r, c = side.shape
        pieces = 1 << ((nj * ni).bit_length() - 1)
        assert r % pieces == 0 and (r // pieces) % 16 == 0
        rs = r // pieces
        in_specs.append(pl.BlockSpec((None, rs, c), lambda j, i: (layer, jnp.minimum(j * ni + i, pieces - 1), 0)))
        out_specs.append(pl.BlockSpec((rs, c), lambda j, i: (jnp.minimum(j * ni + i, pieces - 1), 0)))
        out_shape.append(jax.ShapeDtypeStruct((r, c), BF16))
        args.append(side)
    outs = pl.pallas_call(
        functools.partial(_mm_w32_kernel, nw=nw, cw=cw, act=act, has_side=side is not None),
        grid=(nj, ni), in_specs=in_specs, out_specs=out_specs, out_shape=out_shape,
        scratch_shapes=[pltpu.VMEM((k, tn), BF16)],
        compiler_params=_cparams(2, VMEM_LIMIT_MM), name=name,
    )(*args)
    return outs if side is not None else outs[0]


def _mm_acc_kernel(a_ref, w_ref, o_ref, acc_ref):
    kk = pl.program_id(2)

    @pl.when(kk == 0)
    def _():
        acc_ref[...] = jnp.zeros_like(acc_ref)

    acc_ref[...] += jnp.dot(a_ref[...], w_ref[...], preferred_element_type=F32)

    @pl.when(kk == pl.num_programs(2) - 1)
    def _():
        o_ref[...] = acc_ref[...].astype(o_ref.dtype)


def _mm_acc(a, w, layer, out_dtype, tm, tn, tk, name="mm_acc"):
    m, k = a.shape
    n = w.shape[2]
    tm, tn, tk = _tile(m, tm), _tile(n, tn), _tile(k, tk)
    return pl.pallas_call(
        _mm_acc_kernel,
        grid=(m // tm, n // tn, k // tk),
        in_specs=[pl.BlockSpec((tm, tk), lambda i, j, q: (i, q)),
                  pl.BlockSpec((None, tk, tn), lambda i, j, q: (layer, q, j))],
        out_specs=pl.BlockSpec((tm, tn), lambda i, j, q: (i, j)),
        out_shape=jax.ShapeDtypeStruct((m, n), out_dtype),
        scratch_shapes=[pltpu.VMEM((tm, tn), F32)],
        compiler_params=_cparams(3), name=name,
    )(a, w)


def _ln_kernel(*refs, alpha, has_res, n_out):
    x_ref = refs[0]
    r_ref = refs[1] if has_res else None
    g_ref, b_ref = refs[1 + has_res], refs[2 + has_res]
    outs = refs[3 + has_res:3 + has_res + n_out]
    x = x_ref[...].astype(F32)
    if has_res:
        x = alpha * x + r_ref[...].astype(F32)
    y = _layer_norm(x, g_ref[...], b_ref[...])
    for o_ref in outs:
        o_ref[...] = y.astype(o_ref.dtype)


def _ln_rows(x, res, g, b, alpha, tm, in_row0, nrows, out_dtypes, name="ln"):
    d = x.shape[1]
    tm = _tile(nrows, tm, 8)
    assert in_row0 % tm == 0
    ib = in_row0 // tm
    row_in = pl.BlockSpec((tm, d), lambda i: (ib + i, 0))
    row_out = pl.BlockSpec((tm, d), lambda i: (i, 0))
    vec = pl.BlockSpec((1, d), lambda i: (0, 0))
    has_res = res is not None
    args = [x] + ([res] if has_res else []) + [g.reshape(1, d), b.reshape(1, d)]
    specs = [row_in] * (1 + has_res) + [vec, vec]
    return pl.pallas_call(
        functools.partial(_ln_kernel, alpha=alpha, has_res=int(has_res), n_out=len(out_dtypes)),
        grid=(nrows // tm,), in_specs=specs, out_specs=[row_out] * len(out_dtypes),
        out_shape=[jax.ShapeDtypeStruct((nrows, d), dt) for dt in out_dtypes],
        compiler_params=_cparams(1), name=name,
    )(*args)


def _ln_in_kernel(xp_ref, xs_ref, g_ref, b_ref, ob_ref, *, n_first):
    x = jnp.where(pl.program_id(0) < n_first, xp_ref[...], xs_ref[...])
    ob_ref[...] = _layer_norm(x, g_ref[...], b_ref[...]).astype(BF16)


def _ln_in(xp, xs, g, b, tm):
    (n_p, d), n_s = xp.shape, xs.shape[0]
    tm = _tile(math.gcd(n_p, n_s), tm, 8)
    npb = n_p // tm
    row = pl.BlockSpec((tm, d), lambda i: (i, 0))
    vec = pl.BlockSpec((1, d), lambda i: (0, 0))
    return pl.pallas_call(
        functools.partial(_ln_in_kernel, n_first=npb),
        grid=((n_p + n_s) // tm,),
        in_specs=[pl.BlockSpec((tm, d), lambda i: (jnp.minimum(i, npb - 1), 0)),
                  pl.BlockSpec((tm, d), lambda i: (jnp.maximum(i - npb, 0), 0)), vec, vec],
        out_specs=row,
        out_shape=jax.ShapeDtypeStruct((n_p + n_s, d), BF16),
        compiler_params=_cparams(1), name="ln_in",
    )(xp, xs, g.reshape(1, d), b.reshape(1, d))


def _merge_kernel(*refs, n_first):
    yp_refs, ys_refs, g_refs, wb_ref, o_ref = refs[0:4], refs[4:8], refs[8:12], refs[12], refs[13]
    first = pl.program_id(1) < n_first
    acc = None
    for n in range(4):
        y = jnp.where(first, yp_refs[n][...], ys_refs[n][...])
        br = jnp.dot(y, wb_ref[n], preferred_element_type=F32)
        term = _sigmoid(g_refs[n][...].astype(F32)) * br
        acc = term if acc is None else acc + term
    o_ref[...] = acc.astype(o_ref.dtype)


def _merge(ys_p, ys_s, gates, wb, layer, tm, tn):
    (n_p, db), n_s = ys_p[0].shape, ys_s[0].shape[0]
    d = wb.shape[3]
    tm, tn = _tile(math.gcd(n_p, n_s), tm), _tile(d, tn)
    nj = d // tn
    npb = n_p // tm
    yp_spec = pl.BlockSpec((tm, db), lambda j, i: (jnp.minimum(i, npb - 1), 0))
    ys_spec = pl.BlockSpec((tm, db), lambda j, i: (jnp.maximum(i - npb, 0), 0))
    g_specs = [pl.BlockSpec((tm, tn), functools.partial(lambda j, i, n: (i, n * nj + j), n=n)) for n in range(4)]
    return pl.pallas_call(
        functools.partial(_merge_kernel, n_first=npb),
        grid=(nj, (n_p + n_s) // tm),
        in_specs=[yp_spec] * 4 + [ys_spec] * 4 + g_specs
                 + [pl.BlockSpec((None, 4, db, tn), lambda j, i: (layer, 0, 0, j))],
        out_specs=pl.BlockSpec((tm, tn), lambda j, i: (i, j)),
        out_shape=jax.ShapeDtypeStruct((n_p + n_s, d), BF16),
        compiler_params=_cparams(2), name="merge",
    )(*ys_p, *ys_s, gates, gates, gates, gates, wb)


def _mixer_a_kernel(a1_ref, a2_ref, hist_ref, cw_ref, cb_ref, g_ref, b_ref, y_ref, nh_ref, zext_ref, acc_ref,
                    *, tb, width, pad):
    hist = width - 1
    nchunk = zext_ref.shape[0]
    t = pl.program_id(1)

    @pl.when(t == 0)
    def _():
        h = hist_ref[0]
        for c in range(nchunk):
            zext_ref[c, 0:pad - hist, :] = jnp.zeros((pad - hist, LANES), F32)
            zext_ref[c, pad - hist:pad, :] = h[:, c * LANES:(c + 1) * LANES]

    z = a1_ref[...] * _sigmoid(a2_ref[...])
    for c in range(nchunk):
        zext_ref[c, pad:pad + tb, :] = z[:, c * LANES:(c + 1) * LANES]

    rc = 64

    def conv_chunk(c, carry):
        w = cw_ref[c]
        bias = cb_ref[c]
        for r0 in range(0, tb, rc):
            acc = jnp.broadcast_to(bias, (rc, LANES))
            for j in range(width):
                acc = acc + w[j:j + 1, :] * zext_ref[c, pl.ds(pad - hist + j + r0, rc), :]
            acc_ref[c, r0:r0 + rc, :] = acc
        return carry

    lax.fori_loop(0, nchunk, conv_chunk, 0)
    y = jnp.concatenate([acc_ref[c] for c in range(nchunk)], axis=1)
    y = _layer_norm(y, g_ref[...], b_ref[...])
    y_ref[...] = (y * _sigmoid(y)).astype(y_ref.dtype)
    nh_ref[0] = jnp.concatenate([zext_ref[c, tb + pad - hist:tb + pad, :] for c in range(nchunk)], axis=1)
    for c in range(nchunk):
        zext_ref[c, 0:pad, :] = zext_ref[c, tb:tb + pad, :]


def _mixer_c_kernel(xb_ref, gate_ref, hist_ref, h0_ref, cw_ref, cb_ref, wa_ref, ba_ref, wx_ref, bx_ref, lam_ref,
                    y_ref, nhist_ref, nh_ref, xext_ref, a_ref, u_ref, hs_ref, hcar_ref, *, tb, width, pad):
    hist = width - 1
    t = pl.program_id(1)
    db = xb_ref.shape[1]

    @pl.when(t == 0)
    def _():
        xext_ref[0:pad - hist, :] = jnp.zeros((pad - hist, db), F32)
        xext_ref[pad - hist:pad, :] = hist_ref[0]
        hcar_ref[...] = h0_ref[0]

    xext_ref[pad:pad + tb, :] = xb_ref[...]
    cw = cw_ref[...]
    xc = jnp.broadcast_to(cb_ref[...], (tb, db))
    for j in range(width):
        xc = xc + cw[j:j + 1, :] * xext_ref[pad - hist + j:pad - hist + j + tb, :]
    xcb = xc.astype(BF16)
    r = _sigmoid(jnp.dot(xcb, wa_ref[...], preferred_element_type=F32) + ba_ref[...])
    i = _sigmoid(jnp.dot(xcb, wx_ref[...], preferred_element_type=F32) + bx_ref[...])
    log_a = (LRU_C * r) * (-_softplus(-lam_ref[...]))
    a = jnp.exp(log_a)
    a_ref[...] = a
    u_ref[...] = jnp.sqrt(-jnp.tanh(log_a) * (a * a + 1.0)) * (i * xc)

    def step(k, h):
        h = a_ref[pl.ds(k, 1), :] * h + u_ref[pl.ds(k, 1), :]
        hs_ref[pl.ds(k, 1), :] = h
        return h

    h = lax.fori_loop(0, tb, step, hcar_ref[...], unroll=8)
    hcar_ref[...] = h
    y_ref[...] = (hs_ref[...] * _gelu(gate_ref[...])).astype(y_ref.dtype)
    nhist_ref[0] = xext_ref[tb + pad - hist:tb + pad, :]
    nh_ref[0] = h
    xext_ref[0:pad, :] = xext_ref[tb:tb + pad, :]


def _mixer_d_kernel(u_ref, v_ref, g_ref, b_ref, ws_ref, bias_ref, y_ref, *maybe_vrows_ref, tb, rows):
    v = _layer_norm(_gelu(v_ref[...]), g_ref[...], b_ref[...])
    for vrows_ref in maybe_vrows_ref:
        vrows_ref[...] = v
    vb = v.astype(BF16)
    npair = v.shape[1] // LANES
    first = _iota((rows, LANES), 1) < HEAD
    zero = jnp.zeros((rows, LANES), BF16)
    for r0 in range(0, tb, rows):
        for p in range(npair):
            ls = slice(p * LANES, (p + 1) * LANES)
            v2 = vb[r0:r0 + rows, ls]
            vbd = jnp.concatenate([jnp.where(first, v2, zero), jnp.where(first, zero, v2)], axis=0)
            s = jnp.dot(ws_ref[p], vbd, preferred_element_type=F32) + bias_ref[:, ls]
            y_ref[r0:r0 + rows, ls] = (_gelu(u_ref[r0:r0 + rows, ls]) * s).astype(y_ref.dtype)


def _segsum(x, ones_bd):
    outs = [jnp.dot(x[:, c * LANES:(c + 1) * LANES].astype(BF16), ones_bd, preferred_element_type=F32)
            for c in range(x.shape[1] // LANES)]
    return jnp.concatenate(outs, axis=1)


def _wkv_chunk(rows, at_c, rt_c, bt_c, kt_c, v_s, pc_c, st_ref, y_s, ap_ref, x_ref, ak_ref, ar_ref, rhs_ref, u_ref):
    c = WKV_CHUNK
    npair = st_ref.shape[0]
    lane = _iota((c, LANES), 1)
    first = lane < HEAD

    def bd(x):
        return jnp.concatenate([jnp.where(first, x, 0.0), jnp.where(first, 0.0, x)], axis=0)

    row = _iota((c, LANES), 0)
    col = jnp.where(first, lane, lane - HEAD)
    strict = row > col
    incl = row >= col
    eye = (_iota((LANES, LANES), 0) == _iota((LANES, LANES), 1)).astype(F32)
    same_head = (_iota((LANES, LANES), 0) < HEAD) == (_iota((LANES, LANES), 1) < HEAD)
    lanes = [slice(p * LANES, (p + 1) * LANES) for p in range(npair)]

    for p, ls in enumerate(lanes):
        g = _dot(jnp.concatenate([at_c[:, ls], rt_c[:, ls]], axis=0),
                 jnp.concatenate([bd(bt_c[:, ls]), bd(kt_c[:, ls])], axis=0), _NT)
        a_pow = bd(jnp.where(strict, g[:c, :LANES], 0.0))
        ap_ref[p] = a_pow.astype(BF16)
        x_ref[p] = eye + a_pow
        ak_ref[p] = jnp.where(strict, g[:c, LANES:], 0.0).astype(BF16)
        ar_ref[p] = jnp.concatenate([jnp.where(incl, g[c:, :LANES], 0.0),
                                     jnp.where(incl, g[c:, LANES:], 0.0)], axis=1).astype(BF16)

    for p in range(npair):
        a_pow = ap_ref[p]
        ap_ref[p] = jnp.dot(a_pow, a_pow, preferred_element_type=F32).astype(BF16)
    for step in range(5):
        for p in range(npair):
            x = x_ref[p]
            a_pow = ap_ref[p]
            if step < 4:
                both = jnp.dot(jnp.concatenate([x.astype(BF16), a_pow], axis=0), a_pow, preferred_element_type=F32)
                x_ref[p] = x + both[:LANES]
                ap_ref[p] = both[LANES:].astype(BF16)
            else:
                x_ref[p] = x + jnp.dot(x.astype(BF16), a_pow, preferred_element_type=F32)

    for p, ls in enumerate(lanes):
        rhs_ref[p] = _dot(at_c[:, ls], st_ref[p], _NT) + jnp.dot(ak_ref[p], bd(v_s[rows, ls]).astype(BF16),
                                                                  preferred_element_type=F32)
    for p in range(npair):
        x = x_ref[p]
        u_ref[p] = _dot(x[:c] + x[c:], bd(rhs_ref[p]))
    for p, ls in enumerate(lanes):
        u2 = u_ref[p]
        v2 = v_s[rows, ls]
        s0 = st_ref[p]
        y2 = _dot(rt_c[:, ls], s0, _NT) + jnp.dot(ar_ref[p], jnp.concatenate([bd(u2), bd(v2)], axis=0).astype(BF16),
                                                  preferred_element_type=F32)
        uvt = jnp.transpose(jnp.concatenate([u2, v2], axis=0))
        upd = _dot(uvt, jnp.concatenate([bt_c[:, ls], kt_c[:, ls]], axis=0))
        y_s[rows, ls] = y2
        st_ref[p] = jnp.where(same_head, (s0 + upd) * pc_c[:, ls], 0.0)


def _mixer_b_kernel(pr_ref, pk_ref, pv_ref, pl_ref, sh_ref, shl_ref, s0_ref,
                    mu_ref, mul_ref, w0_ref, ww2_ref, a0_ref, wa2_ref, wg2_ref, kk_ref, ka_ref, rk_ref, gg_ref, gb_ref,
                    y_ref, nsh_ref, nshl_ref, ns_ref,
                    pext_ref, lext_ref, st_ref, r_s, k_s, v_s, kk_s, b_s, ld_s, y_s,
                    at_c, rt_c, bt_c, kt_c, pc_c, ap_ref, x_ref, ak_ref, ar_ref, rhs_ref, u_ref, *, tb, pad):
    t = pl.program_id(1)
    db = pr_ref.shape[1]

    @pl.when(t == 0)
    def _():
        pext_ref[0:pad, :] = jnp.broadcast_to(sh_ref[0], (pad, 3 * db))
        lext_ref[0:pad, :] = jnp.broadcast_to(shl_ref[0], (pad, lext_ref.shape[1]))
        st_ref[...] = s0_ref[0]

    pext_ref[pad:pad + tb, 0:db] = pr_ref[...]
    pext_ref[pad:pad + tb, db:2 * db] = pk_ref[...]
    pext_ref[pad:pad + tb, 2 * db:3 * db] = pv_ref[...]
    lext_ref[pad:pad + tb, :] = pl_ref[...]
    p = pext_ref[pad:pad + tb, :]
    ps = p + mu_ref[...] * (pext_ref[pad - 1:pad - 1 + tb, :] - p)
    lo_ = lext_ref[pad:pad + tb, :]
    lo_s = lo_ + mul_ref[...] * (lext_ref[pad - 1:pad - 1 + tb, :] - lo_)
    nsh_ref[0] = pext_ref[pad + tb - 1:pad + tb, :]
    nshl_ref[0] = lext_ref[pad + tb - 1:pad + tb, :]
    pext_ref[0:pad, :] = pext_ref[tb:tb + pad, :]
    lext_ref[0:pad, :] = lext_ref[tb:tb + pad, :]

    r = ps[:, 0:db]
    k = ps[:, db:2 * db]
    v = ps[:, 2 * db:3 * db]
    wa_lo = lo_s[:, 0:LANES]
    g_lo = lo_s[:, LANES:2 * LANES]
    w = -_softplus(-(w0_ref[...] + jnp.dot(_lhs3(jnp.tanh(wa_lo)), ww2_ref[...], preferred_element_type=F32))) - 0.5
    a = _sigmoid(a0_ref[...] + jnp.dot(_lhs3(wa_lo), wa2_ref[...], preferred_element_type=F32))
    gate = jnp.dot(_lhs3(_sigmoid(g_lo)), wg2_ref[...], preferred_element_type=F32)

    ones2 = ((_iota((LANES, LANES), 0) < HEAD) == (_iota((LANES, LANES), 1) < HEAD)).astype(BF16)

    kk = k * kk_ref[...]
    kk = kk * lax.rsqrt(jnp.maximum(_segsum(kk * kk, ones2), 1e-24))
    k = k * (1.0 + (a - 1.0) * ka_ref[...])
    r_s[...] = r
    k_s[...] = k
    v_s[...] = v
    kk_s[...] = kk
    b_s[...] = kk * a
    ld_s[...] = -jnp.exp(w)

    c = WKV_CHUNK
    tri = (_iota((c, c), 0) >= _iota((c, c), 1)).astype(BF16)
    tri3 = jnp.concatenate([tri, tri, tri], axis=1)

    def chunk(ci, carry):
        rows = pl.ds(pl.multiple_of(ci * c, c), c)
        ld = ld_s[rows, :]
        hi = ld.astype(BF16)
        r1 = ld - hi.astype(F32)
        mid = r1.astype(BF16)
        lo = (r1 - mid.astype(F32)).astype(BF16)
        lc = jnp.dot(tri3, jnp.concatenate([hi, mid, lo], axis=0), preferred_element_type=F32)
        pcum = jnp.exp(lc)
        pinv = jnp.exp(-lc)
        at_c[...] = -(kk_s[rows, :] * jnp.exp(lc - ld))
        bt_c[...] = b_s[rows, :] * pinv
        kt_c[...] = k_s[rows, :] * pinv
        rt_c[...] = r_s[rows, :] * pcum
        pc_c[...] = pcum[c - 1:c, :]
        _wkv_chunk(rows, at_c, rt_c, bt_c, kt_c, v_s, pc_c, st_ref, y_s, ap_ref, x_ref, ak_ref, ar_ref, rhs_ref, u_ref)
        return carry

    lax.fori_loop(0, tb // c, chunk, 0)
    ns_ref[0] = st_ref[...]

    y = y_s[...]
    mu_y = _segsum(y, ones2) * (1.0 / HEAD)
    yc = y - mu_y
    var_y = _segsum(yc * yc, ones2) * (1.0 / HEAD)
    yn = yc * lax.rsqrt(var_y + GN_EPS_B) * gg_ref[...] + gb_ref[...]
    bonus = _segsum(r_s[...] * k_s[...] * rk_ref[...], ones2) * v_s[...]
    y_ref[...] = ((yn + bonus) * gate).astype(y_ref.dtype)


def _mixers(proj, lora, row0, n_streams, length, tb, st, prm, want_v, tag):
    db = prm["db"]
    nb = length // tb
    off = row0 // tb
    assert row0 % tb == 0 and length % tb == 0
    grid = (n_streams, nb)

    def rows(width, col):
        return pl.BlockSpec((tb, width), lambda s, t: (off + s * nb + t, col))

    def per_stream(shape):
        nd = len(shape)
        return pl.BlockSpec((1,) + shape, lambda s, t: (s,) + (0,) * nd)

    def whole(arr):
        nd = arr.ndim
        return pl.BlockSpec(arr.shape, lambda s, t: (0,) * nd)

    y_shape = jax.ShapeDtypeStruct((n_streams * length, db), BF16)
    y_spec = pl.BlockSpec((tb, db), lambda s, t: (s * nb + t, 0))
    nchunk = db // LANES

    def call(body, n, in_specs, args, out_specs, out_shape, scratch, name):
        return pl.pallas_call(
            body, grid=grid, in_specs=in_specs, out_specs=out_specs, out_shape=out_shape, scratch_shapes=scratch,
            compiler_params=_cparams(2), name=name + tag)(*args)

    wa = prm["conv_a_w"].shape[0]
    pad_a = 32
    cw_a = prm["conv_a_w"].reshape(wa, nchunk, LANES).transpose(1, 0, 2)
    cb_a = prm["conv_a_b"].reshape(nchunk, 1, LANES)
    y_a, n_a = call(
        functools.partial(_mixer_a_kernel, tb=tb, width=wa, pad=pad_a), 0,
        [rows(db, 0), rows(db, 1), per_stream((wa - 1, db)), whole(cw_a), whole(cb_a),
         whole(prm["ln_a_g"]), whole(prm["ln_a_b"])],
        [proj, proj, st["conv_a"], cw_a, cb_a, prm["ln_a_g"], prm["ln_a_b"]],
        [y_spec, per_stream((wa - 1, db))],
        [y_shape, jax.ShapeDtypeStruct((n_streams, wa - 1, db), F32)],
        [pltpu.VMEM((nchunk, tb + pad_a, LANES), F32), pltpu.VMEM((nchunk, tb, LANES), F32)], "mixer_a")

    pad_b = 8
    nl = lora.shape[1]
    c = WKV_CHUNK
    b_w = [prm[k] for k in ("mu_rkv", "mu_l", "w0", "ww2", "a0", "wa2", "wg2", "k_k", "k_a", "r_k", "gn_g", "gn_b")]
    y_b, n_sh, n_shl, n_wkv = call(
        functools.partial(_mixer_b_kernel, tb=tb, pad=pad_b), 1,
        [rows(db, 2), rows(db, 3), rows(db, 4), pl.BlockSpec((tb, nl), lambda s, t: (off + s * nb + t, 0)),
         per_stream((1, 3 * db)), per_stream((1, nl)), per_stream((nchunk, LANES, LANES))] + [whole(w) for w in b_w],
        [proj, proj, proj, lora, st["shift_rkv"], st["shift_l"], st["wkv"]] + b_w,
        [y_spec, per_stream((1, 3 * db)), per_stream((1, nl)), per_stream((nchunk, LANES, LANES))],
        [y_shape, jax.ShapeDtypeStruct((n_streams, 1, 3 * db), F32), jax.ShapeDtypeStruct((n_streams, 1, nl), F32),
         jax.ShapeDtypeStruct((n_streams, nchunk, LANES, LANES), F32)],
        [pltpu.VMEM((tb + pad_b, 3 * db), F32), pltpu.VMEM((tb + pad_b, nl), F32),
         pltpu.VMEM((nchunk, LANES, LANES), F32)] + [pltpu.VMEM((tb, db), F32)] * 7
        + [pltpu.VMEM((c, db), F32)] * 4 + [pltpu.VMEM((1, db), F32),
                                            pltpu.VMEM((nchunk, LANES, LANES), BF16),
                                            pltpu.VMEM((nchunk, LANES, LANES), F32),
                                            pltpu.VMEM((nchunk, c, LANES), BF16),
                                            pltpu.VMEM((nchunk, c, 2 * LANES), BF16),
                                            pltpu.VMEM((nchunk, c, LANES), F32),
                                            pltpu.VMEM((nchunk, c, LANES), F32)], "mixer_b")

    wc = prm["conv_c_w"].shape[0]
    pad_c = 8
    c_w = [prm[k] for k in ("conv_c_w", "conv_c_b", "lru_wa", "lru_ba", "lru_wx", "lru_bx", "lru_lambda")]
    y_c, n_cc, n_ch = call(
        functools.partial(_mixer_c_kernel, tb=tb, width=wc, pad=pad_c), 2,
        [rows(db, 5), rows(db, 6), per_stream((wc - 1, db)), per_stream((1, db))] + [whole(w) for w in c_w],
        [proj, proj, st["conv_c"], st["lru"]] + c_w,
        [y_spec, per_stream((wc - 1, db)), per_stream((1, db))],
        [y_shape, jax.ShapeDtypeStruct((n_streams, wc - 1, db), F32), jax.ShapeDtypeStruct((n_streams, 1, db), F32)],
        [pltpu.VMEM((tb + pad_c, db), F32)] + [pltpu.VMEM((tb, db), F32)] * 3 + [pltpu.VMEM((1, db), F32)], "mixer_c")

    rows_d = min(tb, MLP_CHUNK)
    ws = prm["ws_pair"] if rows_d == MLP_CHUNK else prm["ws_pair_half"]
    bias = prm["bias_d"][:rows_d]
    d_out_specs, d_out_shape = [y_spec], [y_shape]
    if want_v:
        d_out_specs.append(pl.BlockSpec((tb, db), lambda s, t: (s * nb + t, 0)))
        d_out_shape.append(jax.ShapeDtypeStruct((n_streams * length, db), F32))
    d_out = call(
        functools.partial(_mixer_d_kernel, tb=tb, rows=rows_d), 3,
        [rows(db, 7), rows(db, 8), whole(prm["ln_d_g"]), whole(prm["ln_d_b"]), whole(ws), whole(bias)],
        [proj, proj, prm["ln_d_g"], prm["ln_d_b"], ws, bias],
        d_out_specs, d_out_shape, [], "mixer_d")
    y_d = d_out[0]
    v_rows = d_out[1] if want_v else None

    new = dict(conv_a=n_a, shift_rkv=n_sh, shift_l=n_shl, wkv=n_wkv, conv_c=n_cc, lru=n_ch)
    return (y_a, y_b, y_c, y_d), new, v_rows


def _wkv_to_pairs(s):
    top = jnp.pad(s[:, 0::2], ((0, 0), (0, 0), (0, 0), (0, HEAD)))
    bot = jnp.pad(s[:, 1::2], ((0, 0), (0, 0), (0, 0), (HEAD, 0)))
    return jnp.concatenate([top, bot], axis=2)


def _wkv_from_pairs(z):
    n, hp = z.shape[0], z.shape[1]
    s = jnp.stack([z[:, :, :HEAD, :HEAD], z[:, :, HEAD:, HEAD:]], axis=2)
    return s.reshape(n, 2 * hp, HEAD, HEAD)


def _block_diag(w):
    n, b, _ = w.shape
    tiled = jnp.tile(w.reshape(n * b, b), (1, n))
    same = (jnp.arange(n * b)[:, None] // b) == (jnp.arange(n * b)[None, :] // b)
    return jnp.where(same, tiled, 0.0)


def _row(v):
    return v.reshape(1, -1)


def kernel(x_prompt, x_sample, state_conv_a, state_shift_b, state_wkv_b, state_conv_c, state_lru_c, ln_in_g, ln_in_b, w_in, conv_a_w, conv_a_b, ln_a_g, ln_a_b, mu_b, w0_b, w_w2_b, a0_b, w_a2_b, w_g2_b, k_k_b, k_a_b, r_k_b, gn_b_g, gn_b_b, conv_c_w, conv_c_b, lru_wa, lru_ba, lru_wx, lru_bx, lru_lambda, ln_d_g, ln_d_b, w_s_d, b_s_d, w_branch, w_out, ln1_g, ln1_b, w_up, w_down, ln2_g, ln2_b):
    batch, seq, d = x_prompt.shape
    dec_batch, dec_seq, _ = x_sample.shape
    depth = w_in.shape[0]
    db = d // 4
    d_ff = w_up.shape[2]
    n_heads = db // HEAD
    lw, la, lg = w_w2_b.shape[1], w_a2_b.shape[1], w_g2_b.shape[1]
    n_lora = lw + la + lg
    assert db % LANES == 0 and lw == HEAD and la == HEAD and lg == LANES and n_lora == WCOL
    assert w_s_d.shape[1] == db // HEAD and w_s_d.shape[2] == MLP_CHUNK
    assert dec_seq == STREAM_CHUNK
    tb_p = 256
    assert seq % tb_p == 0
    n_p, n_s = batch * seq, dec_batch * dec_seq
    n_all = n_p + n_s
    alpha = (2 * depth) ** 0.25
    blk_lora = (2 * db + 3 * db) // WCOL
    blk_gate = (9 * db + n_lora) // WCOL

    def mix_src(b):
        return b + jnp.where(b >= blk_lora, 1, 0)

    xb = _ln_in(x_prompt.reshape(n_p, d), x_sample.reshape(n_s, d), ln_in_g, ln_in_b, 256)

    zeros_p = dict(
        conv_a=jnp.zeros((batch, conv_a_w.shape[1] - 1, db), F32),
        shift_rkv=jnp.zeros((batch, 1, 3 * db), F32),
        shift_l=jnp.zeros((batch, 1, n_lora), F32),
        wkv=jnp.zeros((batch, n_heads // 2, LANES, LANES), F32),
        conv_c=jnp.zeros((batch, conv_c_w.shape[1] - 1, db), F32),
        lru=jnp.zeros((batch, 1, db), F32))

    w_branch_b = w_branch.astype(BF16)
    blk = jnp.arange(MLP_CHUNK) // STREAM_CHUNK
    outs_p, outs_s, v_rows_s = [], [], []
    for l in range(depth):
        proj = _mm_w32(xb, w_in, l, 9 * db // WCOL, mix_src, WCOL, MM_TN, MM_TM, F32, name="proj_mix")
        lora = _mm_w32(xb, w_in, l, 1, lambda b: b + blk_lora, WCOL, WCOL, 1024, F32, name="proj_lora")
        gates = _mm_w32(xb, w_in, l, 4 * d // WCOL, lambda b: b + blk_gate, WCOL, MM_TN, MM_TM, BF16,
                        name="proj_gate")

        ws = jnp.where(blk[:, None] >= blk[None, :], w_s_d[l], 0.0).astype(BF16)
        half = STREAM_CHUNK
        zpad = jnp.zeros((HEAD, db), F32)
        prm = dict(
            db=db,
            conv_a_w=conv_a_w[l], conv_a_b=conv_a_b[l], ln_a_g=_row(ln_a_g[l]), ln_a_b=_row(ln_a_b[l]),
            mu_rkv=_row(mu_b[l][:3 * db]), mu_l=_row(mu_b[l][3 * db:]), w0=_row(w0_b[l]),
            ww2=_rhs3(jnp.concatenate([w_w2_b[l], zpad], axis=0)), a0=_row(a0_b[l]),
            wa2=_rhs3(jnp.concatenate([zpad, w_a2_b[l]], axis=0)), wg2=_rhs3(w_g2_b[l]),
            k_k=_row(k_k_b[l]), k_a=_row(k_a_b[l]), r_k=_row(r_k_b[l]), gn_g=_row(gn_b_g[l]), gn_b=_row(gn_b_b[l]),
            conv_c_w=conv_c_w[l], conv_c_b=_row(conv_c_b[l]),
            lru_wa=_block_diag(lru_wa[l]).astype(BF16), lru_ba=_row(lru_ba[l]),
            lru_wx=_block_diag(lru_wx[l]).astype(BF16), lru_bx=_row(lru_bx[l]),
            lru_lambda=_row(lru_lambda[l]),
            ln_d_g=_row(ln_d_g[l]), ln_d_b=_row(ln_d_b[l]),
            ws_pair=jnp.concatenate([ws[0::2], ws[1::2]], axis=2),
            ws_pair_half=jnp.concatenate([ws[0::2, :half, :half], ws[1::2, :half, :half]], axis=2),
            bias_d=jnp.repeat(b_s_d[l].T, HEAD, axis=1),
        )
        st_s = dict(
            conv_a=state_conv_a[l], shift_rkv=state_shift_b[l][:, None, :3 * db],
            shift_l=state_shift_b[l][:, None, 3 * db:], wkv=_wkv_to_pairs(state_wkv_b[l]),
            conv_c=state_conv_c[l], lru=state_lru_c[l][:, None, :])

        ys_p, new_p, _ = _mixers(proj, lora, 0, batch, seq, tb_p, zeros_p, prm, False, "_p")
        ys_s, new_s, v_rows = _mixers(proj, lora, n_p, dec_batch, dec_seq, dec_seq, st_s, prm, True, "_s")
        merged = _merge(ys_p, ys_s, gates, w_branch_b, l, 512, 1024)
        o = _mm_w32(merged, w_out, l, d // WCOL, lambda b: b, WCOL, MM_TN, MM_TM, BF16, name="out_proj")
        xb, = _ln_rows(xb, o, ln1_g[l], ln1_b[l], alpha, 256, 0, n_all, [BF16], name="ln1")
        hdn, w_down_b = _mm_w32(xb, w_up, l, d_ff // WCOL, lambda b: b, WCOL, MM_TN, MM_TM, BF16, act="relu2",
                                side=w_down, name="mlp_up")
        o = _mm_acc(hdn, w_down_b[None], 0, BF16, 1024, 1024, 4096, name="mlp_down")
        if l + 1 < depth:
            xb, = _ln_rows(xb, o, ln2_g[l], ln2_b[l], alpha, 256, 0, n_all, [BF16], name="ln2")
        else:
            y_prompt, = _ln_rows(xb, o, ln2_g[l], ln2_b[l], alpha, 256, 0, n_p, [F32], name="ln2_p")
            y_sample, = _ln_rows(xb, o, ln2_g[l], ln2_b[l], alpha, 256, n_p, n_s, [F32], name="ln2_s")
        outs_p.append(new_p)
        outs_s.append(new_s)
        v_rows_s.append(v_rows.reshape(dec_batch, dec_seq, db))

    def collect(outs):
        conv_a = jnp.stack([o["conv_a"] for o in outs])
        shift = jnp.stack([jnp.concatenate([o["shift_rkv"][:, 0], o["shift_l"][:, 0]], axis=1) for o in outs])
        wkv = jnp.stack([_wkv_from_pairs(o["wkv"]) for o in outs])
        conv_c = jnp.stack([o["conv_c"] for o in outs])
        lru = jnp.stack([o["lru"][:, 0] for o in outs])
        return conv_a, shift, wkv, conv_c, lru

    y_prompt = y_prompt.reshape(batch, seq, d)
    y_sample = y_sample.reshape(dec_batch, dec_seq, d)
    return (y_prompt, y_sample) + collect(outs_p) + collect(outs_s) + (jnp.stack(v_rows_s),)
```

```python
import functools
import math

import jax
import jax.numpy as jnp
from jax import lax
from jax.experimental import pallas as pl
from jax.experimental.pallas import tpu as pltpu

F32 = jnp.float32
BF16 = jnp.bfloat16

LANES = 128
HEAD = 64
WKV_CHUNK = 64
MLP_CHUNK = 128
STREAM_CHUNK = 64
WCOL = 256
MM_TM, MM_TN = 1024, 1024
VMEM_LIMIT = 56 * 1024 * 1024
VMEM_LIMIT_MM = 60 * 1024 * 1024
LN_EPS = 1e-5
GN_EPS_B = 64e-5
LRU_C = 8.0


def _cparams(n_axes, vmem_limit=VMEM_LIMIT):
    return pltpu.CompilerParams(dimension_semantics=("arbitrary",) * n_axes, vmem_limit_bytes=vmem_limit)


def _dot(a, b, dims=(((1,), (0,)), ((), ()))):
    return lax.dot_general(a.astype(BF16), b.astype(BF16), dims, preferred_element_type=F32)


_NT = (((1,), (1,)), ((), ()))


def _hi_lo(x):
    hi = x.astype(BF16)
    lo = (x - hi.astype(F32)).astype(BF16)
    return hi, lo


def _lhs3(x):
    hi, lo = _hi_lo(x)
    return jnp.concatenate([hi, hi, lo], axis=1)


def _rhs3(w):
    hi, lo = _hi_lo(w.astype(F32))
    return jnp.concatenate([hi, lo, hi], axis=0)


def _gelu(x):
    return 0.5 * x * (1.0 + jnp.tanh(math.sqrt(2.0 / math.pi) * (x + 0.044715 * (x * x * x))))


def _sigmoid(x):
    return 0.5 * jnp.tanh(0.5 * x) + 0.5


def _softplus(x):
    return jnp.maximum(x, 0.0) + jnp.log1p(jnp.exp(-jnp.abs(x)))


def _layer_norm(x, g, b, eps=LN_EPS):
    mu = jnp.mean(x, axis=-1, keepdims=True)
    xc = x - mu
    var = jnp.mean(xc * xc, axis=-1, keepdims=True)
    return xc * lax.rsqrt(var + eps) * g + b


def _iota(shape, dim):
    return lax.broadcasted_iota(jnp.int32, shape, dim)


def _tile(extent, target, quantum=LANES):
    best = None
    for cand in range(quantum, min(extent, target) + 1, quantum):
        if extent % cand == 0:
            best = cand
    assert best is not None, (extent, target, quantum)
    return best


def _mm_w32_kernel(a_ref, wb0_ref, *refs, nw, cw, act, rp, n_pieces, has_side):
    p_refs, refs = refs[:nw], refs[nw:]
    if has_side:
        side_ref, o_ref, side_o_ref, wb_ref = refs
        side_o_ref[...] = side_ref[...].astype(BF16)
    else:
        o_ref, wb_ref = refs
    j, i = pl.program_id(0), pl.program_id(1)
    slot = lax.rem(j, 2)

    @pl.when((j == 0) & (i == 0))
    def _():
        wb_ref[0] = wb0_ref[...]

    r0 = pl.multiple_of(jnp.minimum(i, n_pieces - 1) * rp, rp)
    for q in range(nw):
        wb_ref[1 - slot, pl.ds(r0, rp), q * cw:(q + 1) * cw] = p_refs[q][...].astype(BF16)

    acc = jnp.dot(a_ref[...], wb_ref[slot], preferred_element_type=F32)
    if act == "relu2":
        acc = jnp.square(jnp.maximum(acc, 0.0))
    o_ref[...] = acc.astype(o_ref.dtype)


def _src_block(b, start, skip_at):
    s = start + b
    if skip_at is None:
        return s
    return s + (jnp.where(s >= skip_at, 1, 0) if isinstance(s, jax.Array) else int(s >= skip_at))


def _mm_w32(a, w, layer, n_blocks, start, skip_at, cw, tn, tm, out_dtype, act=None, side=None, name="mm"):
    m, k = a.shape
    tm = _tile(m, tm)
    nw = tn // cw
    assert n_blocks % nw == 0 and w.shape[1] == k
    nj, ni = n_blocks // nw, m // tm
    n_pieces = 1 << (ni.bit_length() - 1)
    assert k % n_pieces == 0 and (k // n_pieces) % 16 == 0
    rp = k // n_pieces
    cols0 = [_src_block(q, start, skip_at) * cw for q in range(nw)]
    wb0 = jnp.concatenate([w[layer, :, c0:c0 + cw] for c0 in cols0], axis=1).astype(BF16)

    def piece_map(j, i, q):
        return layer, jnp.minimum(i, n_pieces - 1), _src_block(nw * jnp.minimum(j + 1, nj - 1) + q, start, skip_at)

    in_specs = ([pl.BlockSpec((tm, k), lambda j, i: (i, 0)),
                 pl.BlockSpec((k, tn), lambda j, i: (0, 0), pipeline_mode=pl.Buffered(1))]
                + [pl.BlockSpec((None, rp, cw), functools.partial(piece_map, q=q)) for q in range(nw)])
    out_specs = [pl.BlockSpec((tm, tn), lambda j, i: (i, j))]
    out_shape = [jax.ShapeDtypeStruct((m, n_blocks * cw), out_dtype)]
    args = [a, wb0] + [w] * nw
    if side is not None:
        _, r, c = side.shape
        pieces = 1 << ((nj * ni).bit_length() - 1)
        assert r % pieces == 0 and (r // pieces) % 16 == 0
        rs = r // pieces
        in_specs.append(pl.BlockSpec((None, rs, c), lambda j, i: (layer, jnp.minimum(j * ni + i, pieces - 1), 0)))
        out_specs.append(pl.BlockSpec((rs, c), lambda j, i: (jnp.minimum(j * ni + i, pieces - 1), 0)))
        out_shape.append(jax.ShapeDtypeStruct((r, c), BF16))
        args.append(side)
    outs = pl.pallas_call(
        functools.partial(_mm_w32_kernel, nw=nw, cw=cw, act=act, rp=rp, n_pieces=n_pieces,
                          has_side=side is not None),
        grid=(nj, ni), in_specs=in_specs, out_specs=out_specs, out_shape=out_shape,
        scratch_shapes=[pltpu.VMEM((2, k, tn), BF16)],
        compiler_params=_cparams(2, VMEM_LIMIT_MM), name=name,
    )(*args)
    return outs if side is not None else outs[0]


def _mm_acc_kernel(a_ref, w_ref, o_ref, acc_ref):
    kk = pl.program_id(2)

    @pl.when(kk == 0)
    def _():
        acc_ref[...] = jnp.zeros_like(acc_ref)

    acc_ref[...] += jnp.dot(a_ref[...], w_ref[...], preferred_element_type=F32)

    @pl.when(kk == pl.num_programs(2) - 1)
    def _():
        o_ref[...] = acc_ref[...].astype(o_ref.dtype)


def _mm_acc(a, w, layer, out_dtype, tm, tn, tk, name="mm_acc"):
    m, k = a.shape
    n = w.shape[2]
    tm, tn, tk = _tile(m, tm), _tile(n, tn), _tile(k, tk)
    return pl.pallas_call(
        _mm_acc_kernel,
        grid=(m // tm, n // tn, k // tk),
        in_specs=[pl.BlockSpec((tm, tk), lambda i, j, q: (i, q)),
                  pl.BlockSpec((None, tk, tn), lambda i, j, q: (layer, q, j))],
        out_specs=pl.BlockSpec((tm, tn), lambda i, j, q: (i, j)),
        out_shape=jax.ShapeDtypeStruct((m, n), out_dtype),
        scratch_shapes=[pltpu.VMEM((tm, tn), F32)],
        compiler_params=_cparams(3), name=name,
    )(a, w)


def _ln_kernel(*refs, alpha, has_res, n_out):
    x_ref = refs[0]
    r_ref = refs[1] if has_res else None
    g_ref, b_ref = refs[1 + has_res], refs[2 + has_res]
    outs = refs[3 + has_res:3 + has_res + n_out]
    x = x_ref[...].astype(F32)
    if has_res:
        x = alpha * x + r_ref[...].astype(F32)
    y = _layer_norm(x, g_ref[...], b_ref[...])
    for o_ref in outs:
        o_ref[...] = y.astype(o_ref.dtype)


def _ln_rows(x, res, g, b, alpha, tm, in_row0, nrows, out_dtypes, name="ln"):
    d = x.shape[1]
    tm = _tile(nrows, tm, 8)
    assert in_row0 % tm == 0
    ib = in_row0 // tm
    row_in = pl.BlockSpec((tm, d), lambda i: (ib + i, 0))
    row_out = pl.BlockSpec((tm, d), lambda i: (i, 0))
    vec = pl.BlockSpec((1, d), lambda i: (0, 0))
    has_res = res is not None
    args = [x] + ([res] if has_res else []) + [g.reshape(1, d), b.reshape(1, d)]
    specs = [row_in] * (1 + has_res) + [vec, vec]
    return pl.pallas_call(
        functools.partial(_ln_kernel, alpha=alpha, has_res=int(has_res), n_out=len(out_dtypes)),
        grid=(nrows // tm,), in_specs=specs, out_specs=[row_out] * len(out_dtypes),
        out_shape=[jax.ShapeDtypeStruct((nrows, d), dt) for dt in out_dtypes],
        compiler_params=_cparams(1), name=name,
    )(*args)


def _ln_in_kernel(xp_ref, xs_ref, g_ref, b_ref, ob_ref, *, n_first):
    x = jnp.where(pl.program_id(0) < n_first, xp_ref[...], xs_ref[...])
    ob_ref[...] = _layer_norm(x, g_ref[...], b_ref[...]).astype(BF16)


def _ln_in(xp, xs, g, b, tm):
    (n_p, d), n_s = xp.shape, xs.shape[0]
    tm = _tile(math.gcd(n_p, n_s), tm, 8)
    npb = n_p // tm
    row = pl.BlockSpec((tm, d), lambda i: (i, 0))
    vec = pl.BlockSpec((1, d), lambda i: (0, 0))
    return pl.pallas_call(
        functools.partial(_ln_in_kernel, n_first=npb),
        grid=((n_p + n_s) // tm,),
        in_specs=[pl.BlockSpec((tm, d), lambda i: (jnp.minimum(i, npb - 1), 0)),
                  pl.BlockSpec((tm, d), lambda i: (jnp.maximum(i - npb, 0), 0)), vec, vec],
        out_specs=row,
        out_shape=jax.ShapeDtypeStruct((n_p + n_s, d), BF16),
        compiler_params=_cparams(1), name="ln_in",
    )(xp, xs, g.reshape(1, d), b.reshape(1, d))


def _merge_kernel(*refs, n_first):
    yp_refs, ys_refs, g_refs, wb_ref, o_ref = refs[0:4], refs[4:8], refs[8:12], refs[12], refs[13]
    first = pl.program_id(1) < n_first
    acc = None
    for n in range(4):
        y = jnp.where(first, yp_refs[n][...], ys_refs[n][...])
        br = jnp.dot(y, wb_ref[n], preferred_element_type=F32)
        term = _sigmoid(g_refs[n][...].astype(F32)) * br
        acc = term if acc is None else acc + term
    o_ref[...] = acc.astype(o_ref.dtype)


def _merge(ys_p, ys_s, gates, wb, layer, tm, tn):
    (n_p, db), n_s = ys_p[0].shape, ys_s[0].shape[0]
    d = wb.shape[3]
    tm, tn = _tile(math.gcd(n_p, n_s), tm), _tile(d, tn)
    nj = d // tn
    npb = n_p // tm
    yp_spec = pl.BlockSpec((tm, db), lambda j, i: (jnp.minimum(i, npb - 1), 0))
    ys_spec = pl.BlockSpec((tm, db), lambda j, i: (jnp.maximum(i - npb, 0), 0))
    g_specs = [pl.BlockSpec((tm, tn), functools.partial(lambda j, i, n: (i, n * nj + j), n=n)) for n in range(4)]
    return pl.pallas_call(
        functools.partial(_merge_kernel, n_first=npb),
        grid=(nj, (n_p + n_s) // tm),
        in_specs=[yp_spec] * 4 + [ys_spec] * 4 + g_specs
                 + [pl.BlockSpec((None, 4, db, tn), lambda j, i: (layer, 0, 0, j))],
        out_specs=pl.BlockSpec((tm, tn), lambda j, i: (i, j)),
        out_shape=jax.ShapeDtypeStruct((n_p + n_s, d), BF16),
        compiler_params=_cparams(2), name="merge",
    )(*ys_p, *ys_s, gates, gates, gates, gates, wb)


def _mixer_a_kernel(a1_ref, a2_ref, hist_ref, cw_ref, cb_ref, g_ref, b_ref, y_ref, nh_ref, zext_ref, acc_ref,
                    *, tb, width, pad):
    hist = width - 1
    nchunk = zext_ref.shape[0]
    t = pl.program_id(1)

    @pl.when(t == 0)
    def _():
        h = hist_ref[0]
        for c in range(nchunk):
            zext_ref[c, 0:pad - hist, :] = jnp.zeros((pad - hist, LANES), F32)
            zext_ref[c, pad - hist:pad, :] = h[:, c * LANES:(c + 1) * LANES]

    z = a1_ref[...] * _sigmoid(a2_ref[...])
    for c in range(nchunk):
        zext_ref[c, pad:pad + tb, :] = z[:, c * LANES:(c + 1) * LANES]

    rc = 64

    def conv_chunk(c, carry):
        w = cw_ref[c]
        bias = cb_ref[c]
        for r0 in range(0, tb, rc):
            acc = jnp.broadcast_to(bias, (rc, LANES))
            for j in range(width):
                acc = acc + w[j:j + 1, :] * zext_ref[c, pl.ds(pad - hist + j + r0, rc), :]
            acc_ref[c, r0:r0 + rc, :] = acc
        return carry

    lax.fori_loop(0, nchunk, conv_chunk, 0)
    y = jnp.concatenate([acc_ref[c] for c in range(nchunk)], axis=1)
    y = _layer_norm(y, g_ref[...], b_ref[...])
    y_ref[...] = (y * _sigmoid(y)).astype(y_ref.dtype)
    nh_ref[0] = jnp.concatenate([zext_ref[c, tb + pad - hist:tb + pad, :] for c in range(nchunk)], axis=1)
    for c in range(nchunk):
        zext_ref[c, 0:pad, :] = zext_ref[c, tb:tb + pad, :]


def _mixer_c_kernel(xb_ref, gate_ref, hist_ref, h0_ref, cw_ref, cb_ref, wa_ref, ba_ref, wx_ref, bx_ref, lam_ref,
                    y_ref, nhist_ref, nh_ref, xext_ref, a_ref, u_ref, hs_ref, hcar_ref, *, tb, width, pad):
    hist = width - 1
    t = pl.program_id(1)
    db = xb_ref.shape[1]

    @pl.when(t == 0)
    def _():
        xext_ref[0:pad - hist, :] = jnp.zeros((pad - hist, db), F32)
        xext_ref[pad - hist:pad, :] = hist_ref[0]
        hcar_ref[...] = h0_ref[0]

    xext_ref[pad:pad + tb, :] = xb_ref[...]
    cw = cw_ref[...]
    xc = jnp.broadcast_to(cb_ref[...], (tb, db))
    for j in range(width):
        xc = xc + cw[j:j + 1, :] * xext_ref[pad - hist + j:pad - hist + j + tb, :]
    xcb = xc.astype(BF16)
    r = _sigmoid(jnp.dot(xcb, wa_ref[...], preferred_element_type=F32) + ba_ref[...])
    i = _sigmoid(jnp.dot(xcb, wx_ref[...], preferred_element_type=F32) + bx_ref[...])
    log_a = (LRU_C * r) * (-_softplus(-lam_ref[...]))
    a = jnp.exp(log_a)
    a_ref[...] = a
    u_ref[...] = jnp.sqrt(-jnp.tanh(log_a) * (a * a + 1.0)) * (i * xc)

    def step(k, h):
        h = a_ref[pl.ds(k, 1), :] * h + u_ref[pl.ds(k, 1), :]
        hs_ref[pl.ds(k, 1), :] = h
        return h

    h = lax.fori_loop(0, tb, step, hcar_ref[...], unroll=8)
    hcar_ref[...] = h
    y_ref[...] = (hs_ref[...] * _gelu(gate_ref[...])).astype(y_ref.dtype)
    nhist_ref[0] = xext_ref[tb + pad - hist:tb + pad, :]
    nh_ref[0] = h
    xext_ref[0:pad, :] = xext_ref[tb:tb + pad, :]


def _mixer_d_kernel(u_ref, v_ref, g_ref, b_ref, ws_ref, bias_ref, y_ref, *maybe_vrows_ref, tb, rows):
    v = _layer_norm(_gelu(v_ref[...]), g_ref[...], b_ref[...])
    for vrows_ref in maybe_vrows_ref:
        vrows_ref[...] = v
    vb = v.astype(BF16)
    npair = v.shape[1] // LANES
    first = _iota((rows, LANES), 1) < HEAD
    zero = jnp.zeros((rows, LANES), BF16)
    for r0 in range(0, tb, rows):
        for p in range(npair):
            ls = slice(p * LANES, (p + 1) * LANES)
            v2 = vb[r0:r0 + rows, ls]
            vbd = jnp.concatenate([jnp.where(first, v2, zero), jnp.where(first, zero, v2)], axis=0)
            s = jnp.dot(ws_ref[p], vbd, preferred_element_type=F32) + bias_ref[:, ls]
            y_ref[r0:r0 + rows, ls] = (_gelu(u_ref[r0:r0 + rows, ls]) * s).astype(y_ref.dtype)


def _segsum(x, ones_bd):
    outs = [jnp.dot(x[:, c * LANES:(c + 1) * LANES].astype(BF16), ones_bd, preferred_element_type=F32)
            for c in range(x.shape[1] // LANES)]
    return jnp.concatenate(outs, axis=1)


def _wkv_chunk(rows, at_c, rt_c, bt_c, kt_c, v_s, pc_c, st_ref, y_s, ap_ref, x_ref, ak_ref, ar_ref, rhs_ref, u_ref):
    c = WKV_CHUNK
    npair = st_ref.shape[0]
    lane = _iota((c, LANES), 1)
    first = lane < HEAD

    def bd(x):
        return jnp.concatenate([jnp.where(first, x, 0.0), jnp.where(first, 0.0, x)], axis=0)

    row = _iota((c, LANES), 0)
    col = jnp.where(first, lane, lane - HEAD)
    strict = row > col
    incl = row >= col
    eye = (_iota((LANES, LANES), 0) == _iota((LANES, LANES), 1)).astype(F32)
    same_head = (_iota((LANES, LANES), 0) < HEAD) == (_iota((LANES, LANES), 1) < HEAD)
    lanes = [slice(p * LANES, (p + 1) * LANES) for p in range(npair)]

    for p, ls in enumerate(lanes):
        g = _dot(jnp.concatenate([at_c[:, ls], rt_c[:, ls]], axis=0),
                 jnp.concatenate([bd(bt_c[:, ls]), bd(kt_c[:, ls])], axis=0), _NT)
        a_pow = bd(jnp.where(strict, g[:c, :LANES], 0.0))
        ap_ref[p] = a_pow.astype(BF16)
        x_ref[p] = eye + a_pow
        ak_ref[p] = jnp.where(strict, g[:c, LANES:], 0.0).astype(BF16)
        ar_ref[p] = jnp.concatenate([jnp.where(incl, g[c:, :LANES], 0.0),
                                     jnp.where(incl, g[c:, LANES:], 0.0)], axis=1).astype(BF16)

    for p in range(npair):
        a_pow = ap_ref[p]
        ap_ref[p] = jnp.dot(a_pow, a_pow, preferred_element_type=F32).astype(BF16)
    for step in range(5):
        for p in range(npair):
            x = x_ref[p]
            a_pow = ap_ref[p]
            if step < 4:
                both = jnp.dot(jnp.concatenate([x.astype(BF16), a_pow], axis=0), a_pow, preferred_element_type=F32)
                x_ref[p] = x + both[:LANES]
                ap_ref[p] = both[LANES:].astype(BF16)
            else:
                x_ref[p] = x + jnp.dot(x.astype(BF16), a_pow, preferred_element_type=F32)

    for p, ls in enumerate(lanes):
        rhs_ref[p] = _dot(at_c[:, ls], st_ref[p], _NT) + jnp.dot(ak_ref[p], bd(v_s[rows, ls]).astype(BF16),
                                                                  preferred_element_type=F32)
    for p in range(npair):
        x = x_ref[p]
        u_ref[p] = _dot(x[:c] + x[c:], bd(rhs_ref[p]))
    for p, ls in enumerate(lanes):
        u2 = u_ref[p]
        v2 = v_s[rows, ls]
        s0 = st_ref[p]
        y2 = _dot(rt_c[:, ls], s0, _NT) + jnp.dot(ar_ref[p], jnp.concatenate([bd(u2), bd(v2)], axis=0).astype(BF16),
                                                  preferred_element_type=F32)
        uvt = jnp.transpose(jnp.concatenate([u2, v2], axis=0))
        upd = _dot(uvt, jnp.concatenate([bt_c[:, ls], kt_c[:, ls]], axis=0))
        y_s[rows, ls] = y2
        st_ref[p] = jnp.where(same_head, (s0 + upd) * pc_c[:, ls], 0.0)


def _mixer_b_kernel(pr_ref, pk_ref, pv_ref, pl_ref, sh_ref, shl_ref, s0_ref,
                    mu_ref, mul_ref, w0_ref, ww2_ref, a0_ref, wa2_ref, wg2_ref, kk_ref, ka_ref, rk_ref, gg_ref, gb_ref,
                    y_ref, nsh_ref, nshl_ref, ns_ref,
                    pext_ref, lext_ref, st_ref, r_s, k_s, v_s, kk_s, b_s, ld_s, y_s,
                    at_c, rt_c, bt_c, kt_c, pc_c, ap_ref, x_ref, ak_ref, ar_ref, rhs_ref, u_ref, *, tb, pad):
    t = pl.program_id(1)
    db = pr_ref.shape[1]

    @pl.when(t == 0)
    def _():
        pext_ref[0:pad, :] = jnp.broadcast_to(sh_ref[0], (pad, 3 * db))
        lext_ref[0:pad, :] = jnp.broadcast_to(shl_ref[0], (pad, lext_ref.shape[1]))
        st_ref[...] = s0_ref[0]

    pext_ref[pad:pad + tb, 0:db] = pr_ref[...]
    pext_ref[pad:pad + tb, db:2 * db] = pk_ref[...]
    pext_ref[pad:pad + tb, 2 * db:3 * db] = pv_ref[...]
    lext_ref[pad:pad + tb, :] = pl_ref[...]
    p = pext_ref[pad:pad + tb, :]
    ps = p + mu_ref[...] * (pext_ref[pad - 1:pad - 1 + tb, :] - p)
    lo_ = lext_ref[pad:pad + tb, :]
    lo_s = lo_ + mul_ref[...] * (lext_ref[pad - 1:pad - 1 + tb, :] - lo_)
    nsh_ref[0] = pext_ref[pad + tb - 1:pad + tb, :]
    nshl_ref[0] = lext_ref[pad + tb - 1:pad + tb, :]
    pext_ref[0:pad, :] = pext_ref[tb:tb + pad, :]
    lext_ref[0:pad, :] = lext_ref[tb:tb + pad, :]

    r = ps[:, 0:db]
    k = ps[:, db:2 * db]
    v = ps[:, 2 * db:3 * db]
    wa_lo = lo_s[:, 0:LANES]
    g_lo = lo_s[:, LANES:2 * LANES]
    w = -_softplus(-(w0_ref[...] + jnp.dot(_lhs3(jnp.tanh(wa_lo)), ww2_ref[...], preferred_element_type=F32))) - 0.5
    a = _sigmoid(a0_ref[...] + jnp.dot(_lhs3(wa_lo), wa2_ref[...], preferred_element_type=F32))
    gate = jnp.dot(_lhs3(_sigmoid(g_lo)), wg2_ref[...], preferred_element_type=F32)

    ones2 = ((_iota((LANES, LANES), 0) < HEAD) == (_iota((LANES, LANES), 1) < HEAD)).astype(BF16)

    kk = k * kk_ref[...]
    kk = kk * lax.rsqrt(jnp.maximum(_segsum(kk * kk, ones2), 1e-24))
    k = k * (1.0 + (a - 1.0) * ka_ref[...])
    r_s[...] = r
    k_s[...] = k
    v_s[...] = v
    kk_s[...] = kk
    b_s[...] = kk * a
    ld_s[...] = -jnp.exp(w)

    c = WKV_CHUNK
    tri = (_iota((c, c), 0) >= _iota((c, c), 1)).astype(BF16)
    tri3 = jnp.concatenate([tri, tri, tri], axis=1)

    def chunk(ci, carry):
        rows = pl.ds(pl.multiple_of(ci * c, c), c)
        ld = ld_s[rows, :]
        hi = ld.astype(BF16)
        r1 = ld - hi.astype(F32)
        mid = r1.astype(BF16)
        lo = (r1 - mid.astype(F32)).astype(BF16)
        lc = jnp.dot(tri3, jnp.concatenate([hi, mid, lo], axis=0), preferred_element_type=F32)
        pcum = jnp.exp(lc)
        pinv = jnp.exp(-lc)
        at_c[...] = -(kk_s[rows, :] * jnp.exp(lc - ld))
        bt_c[...] = b_s[rows, :] * pinv
        kt_c[...] = k_s[rows, :] * pinv
        rt_c[...] = r_s[rows, :] * pcum
        pc_c[...] = pcum[c - 1:c, :]
        _wkv_chunk(rows, at_c, rt_c, bt_c, kt_c, v_s, pc_c, st_ref, y_s, ap_ref, x_ref, ak_ref, ar_ref, rhs_ref, u_ref)
        return carry

    lax.fori_loop(0, tb // c, chunk, 0)
    ns_ref[0] = st_ref[...]

    y = y_s[...]
    mu_y = _segsum(y, ones2) * (1.0 / HEAD)
    yc = y - mu_y
    var_y = _segsum(yc * yc, ones2) * (1.0 / HEAD)
    yn = yc * lax.rsqrt(var_y + GN_EPS_B) * gg_ref[...] + gb_ref[...]
    bonus = _segsum(r_s[...] * k_s[...] * rk_ref[...], ones2) * v_s[...]
    y_ref[...] = ((yn + bonus) * gate).astype(y_ref.dtype)


def _mixers(proj, lora, row0, n_streams, length, tb, st, prm, want_v, tag):
    db = prm["db"]
    nb = length // tb
    off = row0 // tb
    assert row0 % tb == 0 and length % tb == 0
    grid = (n_streams, nb)

    def rows(width, col):
        return pl.BlockSpec((tb, width), lambda s, t: (off + s * nb + t, col))

    def per_stream(shape):
        nd = len(shape)
        return pl.BlockSpec((1,) + shape, lambda s, t: (s,) + (0,) * nd)

    def whole(arr):
        nd = arr.ndim
        return pl.BlockSpec(arr.shape, lambda s, t: (0,) * nd)

    y_shape = jax.ShapeDtypeStruct((n_streams * length, db), BF16)
    y_spec = pl.BlockSpec((tb, db), lambda s, t: (s * nb + t, 0))
    nchunk = db // LANES

    def call(body, n, in_specs, args, out_specs, out_shape, scratch, name):
        return pl.pallas_call(
            body, grid=grid, in_specs=in_specs, out_specs=out_specs, out_shape=out_shape, scratch_shapes=scratch,
            compiler_params=_cparams(2), name=name + tag)(*args)

    wa = prm["conv_a_w"].shape[0]
    pad_a = 32
    cw_a = prm["conv_a_w"].reshape(wa, nchunk, LANES).transpose(1, 0, 2)
    cb_a = prm["conv_a_b"].reshape(nchunk, 1, LANES)
    y_a, n_a = call(
        functools.partial(_mixer_a_kernel, tb=tb, width=wa, pad=pad_a), 0,
        [rows(db, 0), rows(db, 1), per_stream((wa - 1, db)), whole(cw_a), whole(cb_a),
         whole(prm["ln_a_g"]), whole(prm["ln_a_b"])],
        [proj, proj, st["conv_a"], cw_a, cb_a, prm["ln_a_g"], prm["ln_a_b"]],
        [y_spec, per_stream((wa - 1, db))],
        [y_shape, jax.ShapeDtypeStruct((n_streams, wa - 1, db), F32)],
        [pltpu.VMEM((nchunk, tb + pad_a, LANES), F32), pltpu.VMEM((nchunk, tb, LANES), F32)], "mixer_a")

    pad_b = 8
    nl = lora.shape[1]
    c = WKV_CHUNK
    b_w = [prm[k] for k in ("mu_rkv", "mu_l", "w0", "ww2", "a0", "wa2", "wg2", "k_k", "k_a", "r_k", "gn_g", "gn_b")]
    y_b, n_sh, n_shl, n_wkv = call(
        functools.partial(_mixer_b_kernel, tb=tb, pad=pad_b), 1,
        [rows(db, 2), rows(db, 3), rows(db, 4), pl.BlockSpec((tb, nl), lambda s, t: (off + s * nb + t, 0)),
         per_stream((1, 3 * db)), per_stream((1, nl)), per_stream((nchunk, LANES, LANES))] + [whole(w) for w in b_w],
        [proj, proj, proj, lora, st["shift_rkv"], st["shift_l"], st["wkv"]] + b_w,
        [y_spec, per_stream((1, 3 * db)), per_stream((1, nl)), per_stream((nchunk, LANES, LANES))],
        [y_shape, jax.ShapeDtypeStruct((n_streams, 1, 3 * db), F32), jax.ShapeDtypeStruct((n_streams, 1, nl), F32),
         jax.ShapeDtypeStruct((n_streams, nchunk, LANES, LANES), F32)],
        [pltpu.VMEM((tb + pad_b, 3 * db), F32), pltpu.VMEM((tb + pad_b, nl), F32),
         pltpu.VMEM((nchunk, LANES, LANES), F32)] + [pltpu.VMEM((tb, db), F32)] * 7
        + [pltpu.VMEM((c, db), F32)] * 4 + [pltpu.VMEM((1, db), F32),
                                            pltpu.VMEM((nchunk, LANES, LANES), BF16),
                                            pltpu.VMEM((nchunk, LANES, LANES), F32),
                                            pltpu.VMEM((nchunk, c, LANES), BF16),
                                            pltpu.VMEM((nchunk, c, 2 * LANES), BF16),
                                            pltpu.VMEM((nchunk, c, LANES), F32),
                                            pltpu.VMEM((nchunk, c, LANES), F32)], "mixer_b")

    wc = prm["conv_c_w"].shape[0]
    pad_c = 8
    c_w = [prm[k] for k in ("conv_c_w", "conv_c_b", "lru_wa", "lru_ba", "lru_wx", "lru_bx", "lru_lambda")]
    y_c, n_cc, n_ch = call(
        functools.partial(_mixer_c_kernel, tb=tb, width=wc, pad=pad_c), 2,
        [rows(db, 5), rows(db, 6), per_stream((wc - 1, db)), per_stream((1, db))] + [whole(w) for w in c_w],
        [proj, proj, st["conv_c"], st["lru"]] + c_w,
        [y_spec, per_stream((wc - 1, db)), per_stream((1, db))],
        [y_shape, jax.ShapeDtypeStruct((n_streams, wc - 1, db), F32), jax.ShapeDtypeStruct((n_streams, 1, db), F32)],
        [pltpu.VMEM((tb + pad_c, db), F32)] + [pltpu.VMEM((tb, db), F32)] * 3 + [pltpu.VMEM((1, db), F32)], "mixer_c")

    rows_d = min(tb, MLP_CHUNK)
    ws = prm["ws_pair"] if rows_d == MLP_CHUNK else prm["ws_pair_half"]
    bias = prm["bias_d"][:rows_d]
    d_out_specs, d_out_shape = [y_spec], [y_shape]
    if want_v:
        d_out_specs.append(pl.BlockSpec((tb, db), lambda s, t: (s * nb + t, 0)))
        d_out_shape.append(jax.ShapeDtypeStruct((n_streams * length, db), F32))
    d_out = call(
        functools.partial(_mixer_d_kernel, tb=tb, rows=rows_d), 3,
        [rows(db, 7), rows(db, 8), whole(prm["ln_d_g"]), whole(prm["ln_d_b"]), whole(ws), whole(bias)],
        [proj, proj, prm["ln_d_g"], prm["ln_d_b"], ws, bias],
        d_out_specs, d_out_shape, [], "mixer_d")
    y_d = d_out[0]
    v_rows = d_out[1] if want_v else None

    new = dict(conv_a=n_a, shift_rkv=n_sh, shift_l=n_shl, wkv=n_wkv, conv_c=n_cc, lru=n_ch)
    return (y_a, y_b, y_c, y_d), new, v_rows


def _wkv_to_pairs(s):
    top = jnp.pad(s[:, 0::2], ((0, 0), (0, 0), (0, 0), (0, HEAD)))
    bot = jnp.pad(s[:, 1::2], ((0, 0), (0, 0), (0, 0), (HEAD, 0)))
    return jnp.concatenate([top, bot], axis=2)


def _wkv_from_pairs(z):
    n, hp = z.shape[0], z.shape[1]
    s = jnp.stack([z[:, :, :HEAD, :HEAD], z[:, :, HEAD:, HEAD:]], axis=2)
    return s.reshape(n, 2 * hp, HEAD, HEAD)


def _block_diag(w):
    n, b, _ = w.shape
    tiled = jnp.tile(w.reshape(n * b, b), (1, n))
    same = (jnp.arange(n * b)[:, None] // b) == (jnp.arange(n * b)[None, :] // b)
    return jnp.where(same, tiled, 0.0)


def _row(v):
    return v.reshape(1, -1)


def kernel(x_prompt, x_sample, state_conv_a, state_shift_b, state_wkv_b, state_conv_c, state_lru_c, ln_in_g, ln_in_b, w_in, conv_a_w, conv_a_b, ln_a_g, ln_a_b, mu_b, w0_b, w_w2_b, a0_b, w_a2_b, w_g2_b, k_k_b, k_a_b, r_k_b, gn_b_g, gn_b_b, conv_c_w, conv_c_b, lru_wa, lru_ba, lru_wx, lru_bx, lru_lambda, ln_d_g, ln_d_b, w_s_d, b_s_d, w_branch, w_out, ln1_g, ln1_b, w_up, w_down, ln2_g, ln2_b):
    batch, seq, d = x_prompt.shape
    dec_batch, dec_seq, _ = x_sample.shape
    depth = w_in.shape[0]
    db = d // 4
    d_ff = w_up.shape[2]
    n_heads = db // HEAD
    lw, la, lg = w_w2_b.shape[1], w_a2_b.shape[1], w_g2_b.shape[1]
    n_lora = lw + la + lg
    assert db % LANES == 0 and lw == HEAD and la == HEAD and lg == LANES and n_lora == WCOL
    assert w_s_d.shape[1] == db // HEAD and w_s_d.shape[2] == MLP_CHUNK
    assert dec_seq == STREAM_CHUNK
    tb_p = 256
    assert seq % tb_p == 0
    n_p, n_s = batch * seq, dec_batch * dec_seq
    n_all = n_p + n_s
    alpha = (2 * depth) ** 0.25
    blk_lora = (2 * db + 3 * db) // WCOL
    blk_gate = (9 * db + n_lora) // WCOL

    xb = _ln_in(x_prompt.reshape(n_p, d), x_sample.reshape(n_s, d), ln_in_g, ln_in_b, 256)

    zeros_p = dict(
        conv_a=jnp.zeros((batch, conv_a_w.shape[1] - 1, db), F32),
        shift_rkv=jnp.zeros((batch, 1, 3 * db), F32),
        shift_l=jnp.zeros((batch, 1, n_lora), F32),
        wkv=jnp.zeros((batch, n_heads // 2, LANES, LANES), F32),
        conv_c=jnp.zeros((batch, conv_c_w.shape[1] - 1, db), F32),
        lru=jnp.zeros((batch, 1, db), F32))

    w_branch_b = w_branch.astype(BF16)
    blk = jnp.arange(MLP_CHUNK) // STREAM_CHUNK
    outs_p, outs_s, v_rows_s = [], [], []
    for l in range(depth):
        proj = _mm_w32(xb, w_in, l, 9 * db // WCOL, 0, blk_lora, WCOL, MM_TN, MM_TM, F32, name="proj_mix")
        lora = _mm_w32(xb, w_in, l, 1, blk_lora, None, WCOL, WCOL, 1024, F32, name="proj_lora")
        gates = _mm_w32(xb, w_in, l, 4 * d // WCOL, blk_gate, None, WCOL, MM_TN, MM_TM, BF16, name="proj_gate")

        ws = jnp.where(blk[:, None] >= blk[None, :], w_s_d[l], 0.0).astype(BF16)
        half = STREAM_CHUNK
        zpad = jnp.zeros((HEAD, db), F32)
        prm = dict(
            db=db,
            conv_a_w=conv_a_w[l], conv_a_b=conv_a_b[l], ln_a_g=_row(ln_a_g[l]), ln_a_b=_row(ln_a_b[l]),
            mu_rkv=_row(mu_b[l][:3 * db]), mu_l=_row(mu_b[l][3 * db:]), w0=_row(w0_b[l]),
            ww2=_rhs3(jnp.concatenate([w_w2_b[l], zpad], axis=0)), a0=_row(a0_b[l]),
            wa2=_rhs3(jnp.concatenate([zpad, w_a2_b[l]], axis=0)), wg2=_rhs3(w_g2_b[l]),
            k_k=_row(k_k_b[l]), k_a=_row(k_a_b[l]), r_k=_row(r_k_b[l]), gn_g=_row(gn_b_g[l]), gn_b=_row(gn_b_b[l]),
            conv_c_w=conv_c_w[l], conv_c_b=_row(conv_c_b[l]),
            lru_wa=_block_diag(lru_wa[l]).astype(BF16), lru_ba=_row(lru_ba[l]),
            lru_wx=_block_diag(lru_wx[l]).astype(BF16), lru_bx=_row(lru_bx[l]),
            lru_lambda=_row(lru_lambda[l]),
            ln_d_g=_row(ln_d_g[l]), ln_d_b=_row(ln_d_b[l]),
            ws_pair=jnp.concatenate([ws[0::2], ws[1::2]], axis=2),
            ws_pair_half=jnp.concatenate([ws[0::2, :half, :half], ws[1::2, :half, :half]], axis=2),
            bias_d=jnp.repeat(b_s_d[l].T, HEAD, axis=1),
        )
        st_s = dict(
            conv_a=state_conv_a[l], shift_rkv=state_shift_b[l][:, None, :3 * db],
            shift_l=state_shift_b[l][:, None, 3 * db:], wkv=_wkv_to_pairs(state_wkv_b[l]),
            conv_c=state_conv_c[l], lru=state_lru_c[l][:, None, :])

        ys_p, new_p, _ = _mixers(proj, lora, 0, batch, seq, tb_p, zeros_p, prm, False, "_p")
        ys_s, new_s, v_rows = _mixers(proj, lora, n_p, dec_batch, dec_seq, dec_seq, st_s, prm, True, "_s")
        merged = _merge(ys_p, ys_s, gates, w_branch_b, l, 512, 1024)
        o = _mm_w32(merged, w_out, l, d // WCOL, 0, None, WCOL, MM_TN, MM_TM, BF16, name="out_proj")
        xb, = _ln_rows(xb, o, ln1_g[l], ln1_b[l], alpha, 256, 0, n_all, [BF16], name="ln1")
        hdn, w_down_b = _mm_w32(xb, w_up, l, d_ff // WCOL, 0, None, WCOL, MM_TN, MM_TM, BF16, act="relu2",
                                side=w_down, name="mlp_up")
        o = _mm_acc(hdn, w_down_b[None], 0, BF16, 1024, 1024, 4096, name="mlp_down")
        if l + 1 < depth:
            xb, = _ln_rows(xb, o, ln2_g[l], ln2_b[l], alpha, 256, 0, n_all, [BF16], name="ln2")
        else:
            y_prompt, = _ln_rows(xb, o, ln2_g[l], ln2_b[l], alpha, 256, 0, n_p, [F32], name="ln2_p")
            y_sample, = _ln_rows(xb, o, ln2_g[l], ln2_b[l], alpha, 256, n_p, n_s, [F32], name="ln2_s")
        outs_p.append(new_p)
        outs_s.append(new_s)
        v_rows_s.append(v_rows.reshape(dec_batch, dec_seq, db))

    def collect(outs):
        conv_a = jnp.stack([o["conv_a"] for o in outs])
        shift = jnp.stack([jnp.concatenate([o["shift_rkv"][:, 0], o["shift_l"][:, 0]], axis=1) for o in outs])
        wkv = jnp.stack([_wkv_from_pairs(o["wkv"]) for o in outs])
        conv_c = jnp.stack([o["conv_c"] for o in outs])
        lru = jnp.stack([o["lru"][:, 0] for o in outs])
        return conv_a, shift, wkv, conv_c, lru

    y_prompt = y_prompt.reshape(batch, seq, d)
    y_sample = y_sample.reshape(dec_batch, dec_seq, d)
    return (y_prompt, y_sample) + collect(outs_p) + collect(outs_s) + (jnp.stack(v_rows_s),)
```

```python
import functools
import math

import jax
import jax.numpy as jnp
from jax import lax
from jax.experimental import pallas as pl
from jax.experimental.pallas import tpu as pltpu

F32 = jnp.float32
BF16 = jnp.bfloat16

LANES = 128
HEAD = 64
WKV_CHUNK = 64
MLP_CHUNK = 128
STREAM_CHUNK = 64
WCOL = 256
MM_TM, MM_TN = 1024, 1024
VMEM_LIMIT = 56 * 1024 * 1024
VMEM_LIMIT_MM = 60 * 1024 * 1024
LN_EPS = 1e-5
GN_EPS_B = 64e-5
LRU_C = 8.0


def _cparams(n_axes, vmem_limit=VMEM_LIMIT):
    return pltpu.CompilerParams(dimension_semantics=("arbitrary",) * n_axes, vmem_limit_bytes=vmem_limit)


def _dot(a, b, dims=(((1,), (0,)), ((), ()))):
    return lax.dot_general(a.astype(BF16), b.astype(BF16), dims, preferred_element_type=F32)


_NT = (((1,), (1,)), ((), ()))


def _hi_lo(x):
    hi = x.astype(BF16)
    lo = (x - hi.astype(F32)).astype(BF16)
    return hi, lo


def _lhs3(x):
    hi, lo = _hi_lo(x)
    return jnp.concatenate([hi, hi, lo], axis=1)


def _rhs3(w):
    hi, lo = _hi_lo(w.astype(F32))
    return jnp.concatenate([hi, lo, hi], axis=0)


def _gelu(x):
    return 0.5 * x * (1.0 + jnp.tanh(math.sqrt(2.0 / math.pi) * (x + 0.044715 * (x * x * x))))


def _sigmoid(x):
    return 0.5 * jnp.tanh(0.5 * x) + 0.5


def _softplus(x):
    return jnp.maximum(x, 0.0) + jnp.log1p(jnp.exp(-jnp.abs(x)))


def _layer_norm(x, g, b, eps=LN_EPS):
    mu = jnp.mean(x, axis=-1, keepdims=True)
    xc = x - mu
    var = jnp.mean(xc * xc, axis=-1, keepdims=True)
    return xc * lax.rsqrt(var + eps) * g + b


def _iota(shape, dim):
    return lax.broadcasted_iota(jnp.int32, shape, dim)


def _tile(extent, target, quantum=LANES):
    best = None
    for cand in range(quantum, min(extent, target) + 1, quantum):
        if extent % cand == 0:
            best = cand
    assert best is not None, (extent, target, quantum)
    return best


def _mm_w32_kernel(a_ref, wb0_ref, *refs, nw, cw, act, rp, n_pieces, has_side):
    p_refs, refs = refs[:nw], refs[nw:]
    if has_side:
        side_ref, o_ref, side_o_ref, wb_ref = refs
        side_o_ref[...] = side_ref[...].astype(BF16)
    else:
        o_ref, wb_ref = refs
    j, i = pl.program_id(0), pl.program_id(1)
    slot = lax.rem(j, 2)

    @pl.when((j == 0) & (i == 0))
    def _():
        wb_ref[0] = wb0_ref[...]

    r0 = pl.multiple_of(jnp.minimum(i, n_pieces - 1) * rp, rp)
    for q in range(nw):
        wb_ref[1 - slot, pl.ds(r0, rp), q * cw:(q + 1) * cw] = p_refs[q][...].astype(BF16)

    acc = jnp.dot(a_ref[...], wb_ref[slot], preferred_element_type=F32)
    if act == "relu2":
        acc = jnp.square(jnp.maximum(acc, 0.0))
    o_ref[...] = acc.astype(o_ref.dtype)


def _round_kernel(w_ref, o_ref):
    o_ref[...] = w_ref[...].astype(BF16)


def _src_block(b, start, skip_at):
    s = start + b
    if skip_at is None:
        return s
    return s + (jnp.where(s >= skip_at, 1, 0) if isinstance(s, jax.Array) else int(s >= skip_at))


def _mm_w32(a, w, layer, n_blocks, start, skip_at, cw, tn, tm, out_dtype, act=None, side=None, name="mm"):
    m, k = a.shape
    tm = _tile(m, tm)
    nw = tn // cw
    assert n_blocks % nw == 0 and w.shape[1] == k
    nj, ni = n_blocks // nw, m // tm
    n_pieces = 1 << (ni.bit_length() - 1)
    assert k % n_pieces == 0 and (k // n_pieces) % 16 == 0
    rp = k // n_pieces
    wb0 = pl.pallas_call(
        _round_kernel, grid=(nw,),
        in_specs=[pl.BlockSpec((None, k, cw), lambda q: (layer, 0, _src_block(q, start, skip_at)))],
        out_specs=pl.BlockSpec((k, cw), lambda q: (0, q)),
        out_shape=jax.ShapeDtypeStruct((k, tn), BF16),
        compiler_params=_cparams(1), name=name + "_w0",
    )(w)

    def piece_map(j, i, q):
        return layer, jnp.minimum(i, n_pieces - 1), _src_block(nw * jnp.minimum(j + 1, nj - 1) + q, start, skip_at)

    in_specs = ([pl.BlockSpec((tm, k), lambda j, i: (i, 0)),
                 pl.BlockSpec((k, tn), lambda j, i: (0, 0), pipeline_mode=pl.Buffered(1))]
                + [pl.BlockSpec((None, rp, cw), functools.partial(piece_map, q=q)) for q in range(nw)])
    out_specs = [pl.BlockSpec((tm, tn), lambda j, i: (i, j))]
    out_shape = [jax.ShapeDtypeStruct((m, n_blocks * cw), out_dtype)]
    args = [a, wb0] + [w] * nw
    if side is not None:
        _, r, c = side.shape
        pieces = 1 << ((nj * ni).bit_length() - 1)
        assert r % pieces == 0 and (r // pieces) % 16 == 0
        rs = r // pieces
        in_specs.append(pl.BlockSpec((None, rs, c), lambda j, i: (layer, jnp.minimum(j * ni + i, pieces - 1), 0)))
        out_specs.append(pl.BlockSpec((rs, c), lambda j, i: (jnp.minimum(j * ni + i, pieces - 1), 0)))
        out_shape.append(jax.ShapeDtypeStruct((r, c), BF16))
        args.append(side)
    outs = pl.pallas_call(
        functools.partial(_mm_w32_kernel, nw=nw, cw=cw, act=act, rp=rp, n_pieces=n_pieces,
                          has_side=side is not None),
        grid=(nj, ni), in_specs=in_specs, out_specs=out_specs, out_shape=out_shape,
        scratch_shapes=[pltpu.VMEM((2, k, tn), BF16)],
        compiler_params=_cparams(2, VMEM_LIMIT_MM), name=name,
    )(*args)
    return outs if side is not None else outs[0]


def _mm_acc_kernel(a_ref, w_ref, o_ref, acc_ref):
    kk = pl.program_id(2)

    @pl.when(kk == 0)
    def _():
        acc_ref[...] = jnp.zeros_like(acc_ref)

    acc_ref[...] += jnp.dot(a_ref[...], w_ref[...], preferred_element_type=F32)

    @pl.when(kk == pl.num_programs(2) - 1)
    def _():
        o_ref[...] = acc_ref[...].astype(o_ref.dtype)


def _mm_acc(a, w, layer, out_dtype, tm, tn, tk, name="mm_acc"):
    m, k = a.shape
    n = w.shape[2]
    tm, tn, tk = _tile(m, tm), _tile(n, tn), _tile(k, tk)
    return pl.pallas_call(
        _mm_acc_kernel,
        grid=(m // tm, n // tn, k // tk),
        in_specs=[pl.BlockSpec((tm, tk), lambda i, j, q: (i, q)),
                  pl.BlockSpec((None, tk, tn), lambda i, j, q: (layer, q, j))],
        out_specs=pl.BlockSpec((tm, tn), lambda i, j, q: (i, j)),
        out_shape=jax.ShapeDtypeStruct((m, n), out_dtype),
        scratch_shapes=[pltpu.VMEM((tm, tn), F32)],
        compiler_params=_cparams(3), name=name,
    )(a, w)


def _ln_kernel(*refs, alpha, has_res, n_out):
    x_ref = refs[0]
    r_ref = refs[1] if has_res else None
    g_ref, b_ref = refs[1 + has_res], refs[2 + has_res]
    outs = refs[3 + has_res:3 + has_res + n_out]
    x = x_ref[...].astype(F32)
    if has_res:
        x = alpha * x + r_ref[...].astype(F32)
    y = _layer_norm(x, g_ref[...], b_ref[...])
    for o_ref in outs:
        o_ref[...] = y.astype(o_ref.dtype)


def _ln_rows(x, res, g, b, alpha, tm, in_row0, nrows, out_dtypes, name="ln"):
    d = x.shape[1]
    tm = _tile(nrows, tm, 8)
    assert in_row0 % tm == 0
    ib = in_row0 // tm
    row_in = pl.BlockSpec((tm, d), lambda i: (ib + i, 0))
    row_out = pl.BlockSpec((tm, d), lambda i: (i, 0))
    vec = pl.BlockSpec((1, d), lambda i: (0, 0))
    has_res = res is not None
    args = [x] + ([res] if has_res else []) + [g.reshape(1, d), b.reshape(1, d)]
    specs = [row_in] * (1 + has_res) + [vec, vec]
    return pl.pallas_call(
        functools.partial(_ln_kernel, alpha=alpha, has_res=int(has_res), n_out=len(out_dtypes)),
        grid=(nrows // tm,), in_specs=specs, out_specs=[row_out] * len(out_dtypes),
        out_shape=[jax.ShapeDtypeStruct((nrows, d), dt) for dt in out_dtypes],
        compiler_params=_cparams(1), name=name,
    )(*args)


def _ln_in_kernel(xp_ref, xs_ref, g_ref, b_ref, ob_ref, *, n_first):
    x = jnp.where(pl.program_id(0) < n_first, xp_ref[...], xs_ref[...])
    ob_ref[...] = _layer_norm(x, g_ref[...], b_ref[...]).astype(BF16)


def _ln_in(xp, xs, g, b, tm):
    (n_p, d), n_s = xp.shape, xs.shape[0]
    tm = _tile(math.gcd(n_p, n_s), tm, 8)
    npb = n_p // tm
    row = pl.BlockSpec((tm, d), lambda i: (i, 0))
    vec = pl.BlockSpec((1, d), lambda i: (0, 0))
    return pl.pallas_call(
        functools.partial(_ln_in_kernel, n_first=npb),
        grid=((n_p + n_s) // tm,),
        in_specs=[pl.BlockSpec((tm, d), lambda i: (jnp.minimum(i, npb - 1), 0)),
                  pl.BlockSpec((tm, d), lambda i: (jnp.maximum(i - npb, 0), 0)), vec, vec],
        out_specs=row,
        out_shape=jax.ShapeDtypeStruct((n_p + n_s, d), BF16),
        compiler_params=_cparams(1), name="ln_in",
    )(xp, xs, g.reshape(1, d), b.reshape(1, d))


def _merge_kernel(*refs, n_first):
    yp_refs, ys_refs, g_refs, wb_ref, o_ref = refs[0:4], refs[4:8], refs[8:12], refs[12], refs[13]
    first = pl.program_id(1) < n_first
    acc = None
    for n in range(4):
        y = jnp.where(first, yp_refs[n][...], ys_refs[n][...])
        br = jnp.dot(y, wb_ref[n], preferred_element_type=F32)
        term = _sigmoid(g_refs[n][...].astype(F32)) * br
        acc = term if acc is None else acc + term
    o_ref[...] = acc.astype(o_ref.dtype)


def _merge(ys_p, ys_s, gates, wb, layer, tm, tn):
    (n_p, db), n_s = ys_p[0].shape, ys_s[0].shape[0]
    d = wb.shape[3]
    tm, tn = _tile(math.gcd(n_p, n_s), tm), _tile(d, tn)
    nj = d // tn
    npb = n_p // tm
    yp_spec = pl.BlockSpec((tm, db), lambda j, i: (jnp.minimum(i, npb - 1), 0))
    ys_spec = pl.BlockSpec((tm, db), lambda j, i: (jnp.maximum(i - npb, 0), 0))
    g_specs = [pl.BlockSpec((tm, tn), functools.partial(lambda j, i, n: (i, n * nj + j), n=n)) for n in range(4)]
    return pl.pallas_call(
        functools.partial(_merge_kernel, n_first=npb),
        grid=(nj, (n_p + n_s) // tm),
        in_specs=[yp_spec] * 4 + [ys_spec] * 4 + g_specs
                 + [pl.BlockSpec((None, 4, db, tn), lambda j, i: (layer, 0, 0, j))],
        out_specs=pl.BlockSpec((tm, tn), lambda j, i: (i, j)),
        out_shape=jax.ShapeDtypeStruct((n_p + n_s, d), BF16),
        compiler_params=_cparams(2), name="merge",
    )(*ys_p, *ys_s, gates, gates, gates, gates, wb)


def _mixer_a_kernel(a1_ref, a2_ref, hist_ref, cw_ref, cb_ref, g_ref, b_ref, y_ref, nh_ref, zext_ref, acc_ref,
                    *, tb, width, pad):
    hist = width - 1
    nchunk = zext_ref.shape[0]
    t = pl.program_id(1)

    @pl.when(t == 0)
    def _():
        h = hist_ref[0]
        for c in range(nchunk):
            zext_ref[c, 0:pad - hist, :] = jnp.zeros((pad - hist, LANES), F32)
            zext_ref[c, pad - hist:pad, :] = h[:, c * LANES:(c + 1) * LANES]

    z = a1_ref[...] * _sigmoid(a2_ref[...])
    for c in range(nchunk):
        zext_ref[c, pad:pad + tb, :] = z[:, c * LANES:(c + 1) * LANES]

    rc = 64

    def conv_chunk(c, carry):
        w = cw_ref[c]
        bias = cb_ref[c]
        for r0 in range(0, tb, rc):
            acc = jnp.broadcast_to(bias, (rc, LANES))
            for j in range(width):
                acc = acc + w[j:j + 1, :] * zext_ref[c, pl.ds(pad - hist + j + r0, rc), :]
            acc_ref[c, r0:r0 + rc, :] = acc
        return carry

    lax.fori_loop(0, nchunk, conv_chunk, 0)
    y = jnp.concatenate([acc_ref[c] for c in range(nchunk)], axis=1)
    y = _layer_norm(y, g_ref[...], b_ref[...])
    y_ref[...] = (y * _sigmoid(y)).astype(y_ref.dtype)
    nh_ref[0] = jnp.concatenate([zext_ref[c, tb + pad - hist:tb + pad, :] for c in range(nchunk)], axis=1)
    for c in range(nchunk):
        zext_ref[c, 0:pad, :] = zext_ref[c, tb:tb + pad, :]


def _mixer_c_kernel(xb_ref, gate_ref, hist_ref, h0_ref, cw_ref, cb_ref, wa_ref, ba_ref, wx_ref, bx_ref, lam_ref,
                    y_ref, nhist_ref, nh_ref, xext_ref, a_ref, u_ref, hs_ref, hcar_ref, *, tb, width, pad):
    hist = width - 1
    t = pl.program_id(1)
    db = xb_ref.shape[1]

    @pl.when(t == 0)
    def _():
        xext_ref[0:pad - hist, :] = jnp.zeros((pad - hist, db), F32)
        xext_ref[pad - hist:pad, :] = hist_ref[0]
        hcar_ref[...] = h0_ref[0]

    xext_ref[pad:pad + tb, :] = xb_ref[...]
    cw = cw_ref[...]
    xc = jnp.broadcast_to(cb_ref[...], (tb, db))
    for j in range(width):
        xc = xc + cw[j:j + 1, :] * xext_ref[pad - hist + j:pad - hist + j + tb, :]
    xcb = xc.astype(BF16)
    r = _sigmoid(jnp.dot(xcb, wa_ref[...], preferred_element_type=F32) + ba_ref[...])
    i = _sigmoid(jnp.dot(xcb, wx_ref[...], preferred_element_type=F32) + bx_ref[...])
    log_a = (LRU_C * r) * (-_softplus(-lam_ref[...]))
    a = jnp.exp(log_a)
    a_ref[...] = a
    u_ref[...] = jnp.sqrt(-jnp.tanh(log_a) * (a * a + 1.0)) * (i * xc)

    def step(k, h):
        h = a_ref[pl.ds(k, 1), :] * h + u_ref[pl.ds(k, 1), :]
        hs_ref[pl.ds(k, 1), :] = h
        return h

    h = lax.fori_loop(0, tb, step, hcar_ref[...], unroll=8)
    hcar_ref[...] = h
    y_ref[...] = (hs_ref[...] * _gelu(gate_ref[...])).astype(y_ref.dtype)
    nhist_ref[0] = xext_ref[tb + pad - hist:tb + pad, :]
    nh_ref[0] = h
    xext_ref[0:pad, :] = xext_ref[tb:tb + pad, :]


def _mixer_d_kernel(u_ref, v_ref, g_ref, b_ref, ws_ref, bias_ref, y_ref, *maybe_vrows_ref, tb, rows):
    v = _layer_norm(_gelu(v_ref[...]), g_ref[...], b_ref[...])
    for vrows_ref in maybe_vrows_ref:
        vrows_ref[...] = v
    vb = v.astype(BF16)
    npair = v.shape[1] // LANES
    first = _iota((rows, LANES), 1) < HEAD
    zero = jnp.zeros((rows, LANES), BF16)
    for r0 in range(0, tb, rows):
        for p in range(npair):
            ls = slice(p * LANES, (p + 1) * LANES)
            v2 = vb[r0:r0 + rows, ls]
            vbd = jnp.concatenate([jnp.where(first, v2, zero), jnp.where(first, zero, v2)], axis=0)
            s = jnp.dot(ws_ref[p], vbd, preferred_element_type=F32) + bias_ref[:, ls]
            y_ref[r0:r0 + rows, ls] = (_gelu(u_ref[r0:r0 + rows, ls]) * s).astype(y_ref.dtype)


def _segsum(x, ones_bd):
    outs = [jnp.dot(x[:, c * LANES:(c + 1) * LANES].astype(BF16), ones_bd, preferred_element_type=F32)
            for c in range(x.shape[1] // LANES)]
    return jnp.concatenate(outs, axis=1)


def _wkv_chunk(rows, at_c, rt_c, bt_c, kt_c, v_s, pc_c, st_ref, y_s, ap_ref, x_ref, ak_ref, ar_ref, rhs_ref, u_ref):
    c = WKV_CHUNK
    npair = st_ref.shape[0]
    lane = _iota((c, LANES), 1)
    first = lane < HEAD

    def bd(x):
        return jnp.concatenate([jnp.where(first, x, 0.0), jnp.where(first, 0.0, x)], axis=0)

    row = _iota((c, LANES), 0)
    col = jnp.where(first, lane, lane - HEAD)
    strict = row > col
    incl = row >= col
    eye = (_iota((LANES, LANES), 0) == _iota((LANES, LANES), 1)).astype(F32)
    same_head = (_iota((LANES, LANES), 0) < HEAD) == (_iota((LANES, LANES), 1) < HEAD)
    lanes = [slice(p * LANES, (p + 1) * LANES) for p in range(npair)]

    for p, ls in enumerate(lanes):
        g = _dot(jnp.concatenate([at_c[:, ls], rt_c[:, ls]], axis=0),
                 jnp.concatenate([bd(bt_c[:, ls]), bd(kt_c[:, ls])], axis=0), _NT)
        a_pow = bd(jnp.where(strict, g[:c, :LANES], 0.0))
        ap_ref[p] = a_pow.astype(BF16)
        x_ref[p] = eye + a_pow
        ak_ref[p] = jnp.where(strict, g[:c, LANES:], 0.0).astype(BF16)
        ar_ref[p] = jnp.concatenate([jnp.where(incl, g[c:, :LANES], 0.0),
                                     jnp.where(incl, g[c:, LANES:], 0.0)], axis=1).astype(BF16)

    for p in range(npair):
        a_pow = ap_ref[p]
        ap_ref[p] = jnp.dot(a_pow, a_pow, preferred_element_type=F32).astype(BF16)
    for step in range(5):
        for p in range(npair):
            x = x_ref[p]
            a_pow = ap_ref[p]
            if step < 4:
                both = jnp.dot(jnp.concatenate([x.astype(BF16), a_pow], axis=0), a_pow, preferred_element_type=F32)
                x_ref[p] = x + both[:LANES]
                ap_ref[p] = both[LANES:].astype(BF16)
            else:
                x_ref[p] = x + jnp.dot(x.astype(BF16), a_pow, preferred_element_type=F32)

    for p, ls in enumerate(lanes):
        rhs_ref[p] = _dot(at_c[:, ls], st_ref[p], _NT) + jnp.dot(ak_ref[p], bd(v_s[rows, ls]).astype(BF16),
                                                                  preferred_element_type=F32)
    for p in range(npair):
        x = x_ref[p]
        u_ref[p] = _dot(x[:c] + x[c:], bd(rhs_ref[p]))
    for p, ls in enumerate(lanes):
        u2 = u_ref[p]
        v2 = v_s[rows, ls]
        s0 = st_ref[p]
        y2 = _dot(rt_c[:, ls], s0, _NT) + jnp.dot(ar_ref[p], jnp.concatenate([bd(u2), bd(v2)], axis=0).astype(BF16),
                                                  preferred_element_type=F32)
        uvt = jnp.transpose(jnp.concatenate([u2, v2], axis=0))
        upd = _dot(uvt, jnp.concatenate([bt_c[:, ls], kt_c[:, ls]], axis=0))
        y_s[rows, ls] = y2
        st_ref[p] = jnp.where(same_head, (s0 + upd) * pc_c[:, ls], 0.0)


def _mixer_b_kernel(pr_ref, pk_ref, pv_ref, pl_ref, sh_ref, shl_ref, s0_ref,
                    mu_ref, mul_ref, w0_ref, ww2_ref, a0_ref, wa2_ref, wg2_ref, kk_ref, ka_ref, rk_ref, gg_ref, gb_ref,
                    y_ref, nsh_ref, nshl_ref, ns_ref,
                    pext_ref, lext_ref, st_ref, r_s, k_s, v_s, kk_s, b_s, ld_s, y_s,
                    at_c, rt_c, bt_c, kt_c, pc_c, ap_ref, x_ref, ak_ref, ar_ref, rhs_ref, u_ref, *, tb, pad):
    t = pl.program_id(1)
    db = pr_ref.shape[1]

    @pl.when(t == 0)
    def _():
        pext_ref[0:pad, :] = jnp.broadcast_to(sh_ref[0], (pad, 3 * db))
        lext_ref[0:pad, :] = jnp.broadcast_to(shl_ref[0], (pad, lext_ref.shape[1]))
        st_ref[...] = s0_ref[0]

    pext_ref[pad:pad + tb, 0:db] = pr_ref[...]
    pext_ref[pad:pad + tb, db:2 * db] = pk_ref[...]
    pext_ref[pad:pad + tb, 2 * db:3 * db] = pv_ref[...]
    lext_ref[pad:pad + tb, :] = pl_ref[...]
    p = pext_ref[pad:pad + tb, :]
    ps = p + mu_ref[...] * (pext_ref[pad - 1:pad - 1 + tb, :] - p)
    lo_ = lext_ref[pad:pad + tb, :]
    lo_s = lo_ + mul_ref[...] * (lext_ref[pad - 1:pad - 1 + tb, :] - lo_)
    nsh_ref[0] = pext_ref[pad + tb - 1:pad + tb, :]
    nshl_ref[0] = lext_ref[pad + tb - 1:pad + tb, :]
    pext_ref[0:pad, :] = pext_ref[tb:tb + pad, :]
    lext_ref[0:pad, :] = lext_ref[tb:tb + pad, :]

    r = ps[:, 0:db]
    k = ps[:, db:2 * db]
    v = ps[:, 2 * db:3 * db]
    wa_lo = lo_s[:, 0:LANES]
    g_lo = lo_s[:, LANES:2 * LANES]
    w = -_softplus(-(w0_ref[...] + jnp.dot(_lhs3(jnp.tanh(wa_lo)), ww2_ref[...], preferred_element_type=F32))) - 0.5
    a = _sigmoid(a0_ref[...] + jnp.dot(_lhs3(wa_lo), wa2_ref[...], preferred_element_type=F32))
    gate = jnp.dot(_lhs3(_sigmoid(g_lo)), wg2_ref[...], preferred_element_type=F32)

    ones2 = ((_iota((LANES, LANES), 0) < HEAD) == (_iota((LANES, LANES), 1) < HEAD)).astype(BF16)

    kk = k * kk_ref[...]
    kk = kk * lax.rsqrt(jnp.maximum(_segsum(kk * kk, ones2), 1e-24))
    k = k * (1.0 + (a - 1.0) * ka_ref[...])
    r_s[...] = r
    k_s[...] = k
    v_s[...] = v
    kk_s[...] = kk
    b_s[...] = kk * a
    ld_s[...] = -jnp.exp(w)

    c = WKV_CHUNK
    tri = (_iota((c, c), 0) >= _iota((c, c), 1)).astype(BF16)
    tri3 = jnp.concatenate([tri, tri, tri], axis=1)

    def chunk(ci, carry):
        rows = pl.ds(pl.multiple_of(ci * c, c), c)
        ld = ld_s[rows, :]
        hi = ld.astype(BF16)
        r1 = ld - hi.astype(F32)
        mid = r1.astype(BF16)
        lo = (r1 - mid.astype(F32)).astype(BF16)
        lc = jnp.dot(tri3, jnp.concatenate([hi, mid, lo], axis=0), preferred_element_type=F32)
        pcum = jnp.exp(lc)
        pinv = jnp.exp(-lc)
        at_c[...] = -(kk_s[rows, :] * jnp.exp(lc - ld))
        bt_c[...] = b_s[rows, :] * pinv
        kt_c[...] = k_s[rows, :] * pinv
        rt_c[...] = r_s[rows, :] * pcum
        pc_c[...] = pcum[c - 1:c, :]
        _wkv_chunk(rows, at_c, rt_c, bt_c, kt_c, v_s, pc_c, st_ref, y_s, ap_ref, x_ref, ak_ref, ar_ref, rhs_ref, u_ref)
        return carry

    lax.fori_loop(0, tb // c, chunk, 0)
    ns_ref[0] = st_ref[...]

    y = y_s[...]
    mu_y = _segsum(y, ones2) * (1.0 / HEAD)
    yc = y - mu_y
    var_y = _segsum(yc * yc, ones2) * (1.0 / HEAD)
    yn = yc * lax.rsqrt(var_y + GN_EPS_B) * gg_ref[...] + gb_ref[...]
    bonus = _segsum(r_s[...] * k_s[...] * rk_ref[...], ones2) * v_s[...]
    y_ref[...] = ((yn + bonus) * gate).astype(y_ref.dtype)


def _mixers(proj, lora, row0, n_streams, length, tb, st, prm, want_v, tag):
    db = prm["db"]
    nb = length // tb
    off = row0 // tb
    assert row0 % tb == 0 and length % tb == 0
    grid = (n_streams, nb)

    def rows(width, col):
        return pl.BlockSpec((tb, width), lambda s, t: (off + s * nb + t, col))

    def per_stream(shape):
        nd = len(shape)
        return pl.BlockSpec((1,) + shape, lambda s, t: (s,) + (0,) * nd)

    def whole(arr):
        nd = arr.ndim
        return pl.BlockSpec(arr.shape, lambda s, t: (0,) * nd)

    y_shape = jax.ShapeDtypeStruct((n_streams * length, db), BF16)
    y_spec = pl.BlockSpec((tb, db), lambda s, t: (s * nb + t, 0))
    nchunk = db // LANES

    def call(body, n, in_specs, args, out_specs, out_shape, scratch, name):
        return pl.pallas_call(
            body, grid=grid, in_specs=in_specs, out_specs=out_specs, out_shape=out_shape, scratch_shapes=scratch,
            compiler_params=_cparams(2), name=name + tag)(*args)

    wa = prm["conv_a_w"].shape[0]
    pad_a = 32
    cw_a = prm["conv_a_w"].reshape(wa, nchunk, LANES).transpose(1, 0, 2)
    cb_a = prm["conv_a_b"].reshape(nchunk, 1, LANES)
    y_a, n_a = call(
        functools.partial(_mixer_a_kernel, tb=tb, width=wa, pad=pad_a), 0,
        [rows(db, 0), rows(db, 1), per_stream((wa - 1, db)), whole(cw_a), whole(cb_a),
         whole(prm["ln_a_g"]), whole(prm["ln_a_b"])],
        [proj, proj, st["conv_a"], cw_a, cb_a, prm["ln_a_g"], prm["ln_a_b"]],
        [y_spec, per_stream((wa - 1, db))],
        [y_shape, jax.ShapeDtypeStruct((n_streams, wa - 1, db), F32)],
        [pltpu.VMEM((nchunk, tb + pad_a, LANES), F32), pltpu.VMEM((nchunk, tb, LANES), F32)], "mixer_a")

    pad_b = 8
    nl = lora.shape[1]
    c = WKV_CHUNK
    b_w = [prm[k] for k in ("mu_rkv", "mu_l", "w0", "ww2", "a0", "wa2", "wg2", "k_k", "k_a", "r_k", "gn_g", "gn_b")]
    y_b, n_sh, n_shl, n_wkv = call(
        functools.partial(_mixer_b_kernel, tb=tb, pad=pad_b), 1,
        [rows(db, 2), rows(db, 3), rows(db, 4), pl.BlockSpec((tb, nl), lambda s, t: (off + s * nb + t, 0)),
         per_stream((1, 3 * db)), per_stream((1, nl)), per_stream((nchunk, LANES, LANES))] + [whole(w) for w in b_w],
        [proj, proj, proj, lora, st["shift_rkv"], st["shift_l"], st["wkv"]] + b_w,
        [y_spec, per_stream((1, 3 * db)), per_stream((1, nl)), per_stream((nchunk, LANES, LANES))],
        [y_shape, jax.ShapeDtypeStruct((n_streams, 1, 3 * db), F32), jax.ShapeDtypeStruct((n_streams, 1, nl), F32),
         jax.ShapeDtypeStruct((n_streams, nchunk, LANES, LANES), F32)],
        [pltpu.VMEM((tb + pad_b, 3 * db), F32), pltpu.VMEM((tb + pad_b, nl), F32),
         pltpu.VMEM((nchunk, LANES, LANES), F32)] + [pltpu.VMEM((tb, db), F32)] * 7
        + [pltpu.VMEM((c, db), F32)] * 4 + [pltpu.VMEM((1, db), F32),
                                            pltpu.VMEM((nchunk, LANES, LANES), BF16),
                                            pltpu.VMEM((nchunk, LANES, LANES), F32),
                                            pltpu.VMEM((nchunk, c, LANES), BF16),
                                            pltpu.VMEM((nchunk, c, 2 * LANES), BF16),
                                            pltpu.VMEM((nchunk, c, LANES), F32),
                                            pltpu.VMEM((nchunk, c, LANES), F32)], "mixer_b")

    wc = prm["conv_c_w"].shape[0]
    pad_c = 8
    c_w = [prm[k] for k in ("conv_c_w", "conv_c_b", "lru_wa", "lru_ba", "lru_wx", "lru_bx", "lru_lambda")]
    y_c, n_cc, n_ch = call(
        functools.partial(_mixer_c_kernel, tb=tb, width=wc, pad=pad_c), 2,
        [rows(db, 5), rows(db, 6), per_stream((wc - 1, db)), per_stream((1, db))] + [whole(w) for w in c_w],
        [proj, proj, st["conv_c"], st["lru"]] + c_w,
        [y_spec, per_stream((wc - 1, db)), per_stream((1, db))],
        [y_shape, jax.ShapeDtypeStruct((n_streams, wc - 1, db), F32), jax.ShapeDtypeStruct((n_streams, 1, db), F32)],
        [pltpu.VMEM((tb + pad_c, db), F32)] + [pltpu.VMEM((tb, db), F32)] * 3 + [pltpu.VMEM((1, db), F32)], "mixer_c")

    rows_d = min(tb, MLP_CHUNK)
    ws = prm["ws_pair"] if rows_d == MLP_CHUNK else prm["ws_pair_half"]
    bias = prm["bias_d"][:rows_d]
    d_out_specs, d_out_shape = [y_spec], [y_shape]
    if want_v:
        d_out_specs.append(pl.BlockSpec((tb, db), lambda s, t: (s * nb + t, 0)))
        d_out_shape.append(jax.ShapeDtypeStruct((n_streams * length, db), F32))
    d_out = call(
        functools.partial(_mixer_d_kernel, tb=tb, rows=rows_d), 3,
        [rows(db, 7), rows(db, 8), whole(prm["ln_d_g"]), whole(prm["ln_d_b"]), whole(ws), whole(bias)],
        [proj, proj, prm["ln_d_g"], prm["ln_d_b"], ws, bias],
        d_out_specs, d_out_shape, [], "mixer_d")
    y_d = d_out[0]
    v_rows = d_out[1] if want_v else None

    new = dict(conv_a=n_a, shift_rkv=n_sh, shift_l=n_shl, wkv=n_wkv, conv_c=n_cc, lru=n_ch)
    return (y_a, y_b, y_c, y_d), new, v_rows


def _wkv_to_pairs(s):
    top = jnp.pad(s[:, 0::2], ((0, 0), (0, 0), (0, 0), (0, HEAD)))
    bot = jnp.pad(s[:, 1::2], ((0, 0), (0, 0), (0, 0), (HEAD, 0)))
    return jnp.concatenate([top, bot], axis=2)


def _wkv_from_pairs(z):
    n, hp = z.shape[0], z.shape[1]
    s = jnp.stack([z[:, :, :HEAD, :HEAD], z[:, :, HEAD:, HEAD:]], axis=2)
    return s.reshape(n, 2 * hp, HEAD, HEAD)


def _block_diag(w):
    n, b, _ = w.shape
    tiled = jnp.tile(w.reshape(n * b, b), (1, n))
    same = (jnp.arange(n * b)[:, None] // b) == (jnp.arange(n * b)[None, :] // b)
    return jnp.where(same, tiled, 0.0)


def _row(v):
    return v.reshape(1, -1)


def kernel(x_prompt, x_sample, state_conv_a, state_shift_b, state_wkv_b, state_conv_c, state_lru_c, ln_in_g, ln_in_b, w_in, conv_a_w, conv_a_b, ln_a_g, ln_a_b, mu_b, w0_b, w_w2_b, a0_b, w_a2_b, w_g2_b, k_k_b, k_a_b, r_k_b, gn_b_g, gn_b_b, conv_c_w, conv_c_b, lru_wa, lru_ba, lru_wx, lru_bx, lru_lambda, ln_d_g, ln_d_b, w_s_d, b_s_d, w_branch, w_out, ln1_g, ln1_b, w_up, w_down, ln2_g, ln2_b):
    batch, seq, d = x_prompt.shape
    dec_batch, dec_seq, _ = x_sample.shape
    depth = w_in.shape[0]
    db = d // 4
    d_ff = w_up.shape[2]
    n_heads = db // HEAD
    lw, la, lg = w_w2_b.shape[1], w_a2_b.shape[1], w_g2_b.shape[1]
    n_lora = lw + la + lg
    assert db % LANES == 0 and lw == HEAD and la == HEAD and lg == LANES and n_lora == WCOL
    assert w_s_d.shape[1] == db // HEAD and w_s_d.shape[2] == MLP_CHUNK
    assert dec_seq == STREAM_CHUNK
    tb_p = 256
    assert seq % tb_p == 0
    n_p, n_s = batch * seq, dec_batch * dec_seq
    n_all = n_p + n_s
    alpha = (2 * depth) ** 0.25
    blk_lora = (2 * db + 3 * db) // WCOL
    blk_gate = (9 * db + n_lora) // WCOL

    xb = _ln_in(x_prompt.reshape(n_p, d), x_sample.reshape(n_s, d), ln_in_g, ln_in_b, 256)

    zeros_p = dict(
        conv_a=jnp.zeros((batch, conv_a_w.shape[1] - 1, db), F32),
        shift_rkv=jnp.zeros((batch, 1, 3 * db), F32),
        shift_l=jnp.zeros((batch, 1, n_lora), F32),
        wkv=jnp.zeros((batch, n_heads // 2, LANES, LANES), F32),
        conv_c=jnp.zeros((batch, conv_c_w.shape[1] - 1, db), F32),
        lru=jnp.zeros((batch, 1, db), F32))

    w_branch_b = w_branch.astype(BF16)
    blk = jnp.arange(MLP_CHUNK) // STREAM_CHUNK
    outs_p, outs_s, v_rows_s = [], [], []
    for l in range(depth):
        proj = _mm_w32(xb, w_in, l, 9 * db // WCOL, 0, blk_lora, WCOL, MM_TN, MM_TM, F32, name="proj_mix")
        lora = _mm_w32(xb, w_in, l, 1, blk_lora, None, WCOL, WCOL, 1024, F32, name="proj_lora")
        gates = _mm_w32(xb, w_in, l, 4 * d // WCOL, blk_gate, None, WCOL, MM_TN, MM_TM, BF16, name="proj_gate")

        ws = jnp.where(blk[:, None] >= blk[None, :], w_s_d[l], 0.0).astype(BF16)
        half = STREAM_CHUNK
        zpad = jnp.zeros((HEAD, db), F32)
        prm = dict(
            db=db,
            conv_a_w=conv_a_w[l], conv_a_b=conv_a_b[l], ln_a_g=_row(ln_a_g[l]), ln_a_b=_row(ln_a_b[l]),
            mu_rkv=_row(mu_b[l][:3 * db]), mu_l=_row(mu_b[l][3 * db:]), w0=_row(w0_b[l]),
            ww2=_rhs3(jnp.concatenate([w_w2_b[l], zpad], axis=0)), a0=_row(a0_b[l]),
            wa2=_rhs3(jnp.concatenate([zpad, w_a2_b[l]], axis=0)), wg2=_rhs3(w_g2_b[l]),
            k_k=_row(k_k_b[l]), k_a=_row(k_a_b[l]), r_k=_row(r_k_b[l]), gn_g=_row(gn_b_g[l]), gn_b=_row(gn_b_b[l]),
            conv_c_w=conv_c_w[l], conv_c_b=_row(conv_c_b[l]),
            lru_wa=_block_diag(lru_wa[l]).astype(BF16), lru_ba=_row(lru_ba[l]),
            lru_wx=_block_diag(lru_wx[l]).astype(BF16), lru_bx=_row(lru_bx[l]),
            lru_lambda=_row(lru_lambda[l]),
            ln_d_g=_row(ln_d_g[l]), ln_d_b=_row(ln_d_b[l]),
            ws_pair=jnp.concatenate([ws[0::2], ws[1::2]], axis=2),
            ws_pair_half=jnp.concatenate([ws[0::2, :half, :half], ws[1::2, :half, :half]], axis=2),
            bias_d=jnp.repeat(b_s_d[l].T, HEAD, axis=1),
        )
        st_s = dict(
            conv_a=state_conv_a[l], shift_rkv=state_shift_b[l][:, None, :3 * db],
            shift_l=state_shift_b[l][:, None, 3 * db:], wkv=_wkv_to_pairs(state_wkv_b[l]),
            conv_c=state_conv_c[l], lru=state_lru_c[l][:, None, :])

        ys_p, new_p, _ = _mixers(proj, lora, 0, batch, seq, tb_p, zeros_p, prm, False, "_p")
        ys_s, new_s, v_rows = _mixers(proj, lora, n_p, dec_batch, dec_seq, dec_seq, st_s, prm, True, "_s")
        merged = _merge(ys_p, ys_s, gates, w_branch_b, l, 512, 1024)
        o = _mm_w32(merged, w_out, l, d // WCOL, 0, None, WCOL, MM_TN, MM_TM, BF16, name="out_proj")
        xb, = _ln_rows(xb, o, ln1_g[l], ln1_b[l], alpha, 256, 0, n_all, [BF16], name="ln1")
        hdn, w_down_b = _mm_w32(xb, w_up, l, d_ff // WCOL, 0, None, WCOL, MM_TN, MM_TM, BF16, act="relu2",
                                side=w_down, name="mlp_up")
        o = _mm_acc(hdn, w_down_b[None], 0, BF16, 1024, 1024, 4096, name="mlp_down")
        if l + 1 < depth:
            xb, = _ln_rows(xb, o, ln2_g[l], ln2_b[l], alpha, 256, 0, n_all, [BF16], name="ln2")
        else:
            y_prompt, = _ln_rows(xb, o, ln2_g[l], ln2_b[l], alpha, 256, 0, n_p, [F32], name="ln2_p")
            y_sample, = _ln_rows(xb, o, ln2_g[l], ln2_b[l], alpha, 256, n_p, n_s, [F32], name="ln2_s")
        outs_p.append(new_p)
        outs_s.append(new_s)
        v_rows_s.append(v_rows.reshape(dec_batch, dec_seq, db))

    def collect(outs):
        conv_a = jnp.stack([o["conv_a"] for o in outs])
        shift = jnp.stack([jnp.concatenate([o["shift_rkv"][:, 0], o["shift_l"][:, 0]], axis=1) for o in outs])
        wkv = jnp.stack([_wkv_from_pairs(o["wkv"]) for o in outs])
        conv_c = jnp.stack([o["conv_c"] for o in outs])
        lru = jnp.stack([o["lru"][:, 0] for o in outs])
        return conv_a, shift, wkv, conv_c, lru

    y_prompt = y_prompt.reshape(batch, seq, d)
    y_sample = y_sample.reshape(dec_batch, dec_seq, d)
    return (y_prompt, y_sample) + collect(outs_p) + collect(outs_s) + (jnp.stack(v_rows_s),)
```

```python
import functools
import math

import jax
import jax.numpy as jnp
from jax import lax
from jax.experimental import pallas as pl
from jax.experimental.pallas import tpu as pltpu

F32 = jnp.float32
BF16 = jnp.bfloat16

LANES = 128
HEAD = 64
WKV_CHUNK = 64
MLP_CHUNK = 128
STREAM_CHUNK = 64
WCOL = 256
MM_TM, MM_TN = 1024, 1024
VMEM_LIMIT = 56 * 1024 * 1024
VMEM_LIMIT_MM = 60 * 1024 * 1024
LN_EPS = 1e-5
GN_EPS_B = 64e-5
LRU_C = 8.0


def _cparams(n_axes, vmem_limit=VMEM_LIMIT):
    return pltpu.CompilerParams(dimension_semantics=("arbitrary",) * n_axes, vmem_limit_bytes=vmem_limit)


def _dot(a, b, dims=(((1,), (0,)), ((), ()))):
    return lax.dot_general(a.astype(BF16), b.astype(BF16), dims, preferred_element_type=F32)


_NT = (((1,), (1,)), ((), ()))


def _hi_lo(x):
    hi = x.astype(BF16)
    lo = (x - hi.astype(F32)).astype(BF16)
    return hi, lo


def _lhs3(x):
    hi, lo = _hi_lo(x)
    return jnp.concatenate([hi, hi, lo], axis=1)


def _rhs3(w):
    hi, lo = _hi_lo(w.astype(F32))
    return jnp.concatenate([hi, lo, hi], axis=0)


def _gelu(x):
    return 0.5 * x * (1.0 + jnp.tanh(math.sqrt(2.0 / math.pi) * (x + 0.044715 * (x * x * x))))


def _sigmoid(x):
    return 0.5 * jnp.tanh(0.5 * x) + 0.5


def _softplus(x):
    return jnp.maximum(x, 0.0) + jnp.log1p(jnp.exp(-jnp.abs(x)))


def _layer_norm(x, g, b, eps=LN_EPS):
    mu = jnp.mean(x, axis=-1, keepdims=True)
    xc = x - mu
    var = jnp.mean(xc * xc, axis=-1, keepdims=True)
    return xc * lax.rsqrt(var + eps) * g + b


def _iota(shape, dim):
    return lax.broadcasted_iota(jnp.int32, shape, dim)


def _tile(extent, target, quantum=LANES):
    best = None
    for cand in range(quantum, min(extent, target) + 1, quantum):
        if extent % cand == 0:
            best = cand
    assert best is not None, (extent, target, quantum)
    return best


def _mm_w32_kernel(a_ref, wb0_ref, *refs, nw, cw, act, rp, n_pieces, has_side):
    p_refs, refs = refs[:nw], refs[nw:]
    if has_side:
        side_ref, o_ref, side_o_ref, wb_ref = refs
        side_o_ref[...] = side_ref[...].astype(BF16)
    else:
        o_ref, wb_ref = refs
    j, i = pl.program_id(0), pl.program_id(1)
    slot = lax.rem(j, 2)

    @pl.when((j == 0) & (i == 0))
    def _():
        wb_ref[0] = wb0_ref[...]

    acc = jnp.dot(a_ref[...], wb_ref[slot], preferred_element_type=F32)
    if act == "relu2":
        acc = jnp.square(jnp.maximum(acc, 0.0))
    o_ref[...] = acc.astype(o_ref.dtype)

    r0 = pl.multiple_of(jnp.minimum(i, n_pieces - 1) * rp, rp)
    for q in range(nw):
        wb_ref[1 - slot, pl.ds(r0, rp), q * cw:(q + 1) * cw] = p_refs[q][...].astype(BF16)


def _round_kernel(w_ref, o_ref):
    o_ref[...] = w_ref[...].astype(BF16)


def _src_block(b, start, skip_at):
    s = start + b
    if skip_at is None:
        return s
    return s + (jnp.where(s >= skip_at, 1, 0) if isinstance(s, jax.Array) else int(s >= skip_at))


def _mm_w32(a, w, layer, n_blocks, start, skip_at, cw, tn, tm, out_dtype, act=None, side=None, name="mm"):
    m, k = a.shape
    tm = _tile(m, tm)
    nw = tn // cw
    assert n_blocks % nw == 0 and w.shape[1] == k
    nj, ni = n_blocks // nw, m // tm
    n_pieces = 1 << (ni.bit_length() - 1)
    assert k % n_pieces == 0 and (k // n_pieces) % 16 == 0
    rp = k // n_pieces
    wb0 = pl.pallas_call(
        _round_kernel, grid=(nw,),
        in_specs=[pl.BlockSpec((None, k, cw), lambda q: (layer, 0, _src_block(q, start, skip_at)))],
        out_specs=pl.BlockSpec((k, cw), lambda q: (0, q)),
        out_shape=jax.ShapeDtypeStruct((k, tn), BF16),
        compiler_params=_cparams(1), name=name + "_w0",
    )(w)

    def piece_map(j, i, q):
        return layer, jnp.minimum(i, n_pieces - 1), _src_block(nw * jnp.minimum(j + 1, nj - 1) + q, start, skip_at)

    in_specs = ([pl.BlockSpec((tm, k), lambda j, i: (i, 0)),
                 pl.BlockSpec((k, tn), lambda j, i: (0, 0), pipeline_mode=pl.Buffered(1))]
                + [pl.BlockSpec((None, rp, cw), functools.partial(piece_map, q=q)) for q in range(nw)])
    out_specs = [pl.BlockSpec((tm, tn), lambda j, i: (i, j))]
    out_shape = [jax.ShapeDtypeStruct((m, n_blocks * cw), out_dtype)]
    args = [a, wb0] + [w] * nw
    if side is not None:
        _, r, c = side.shape
        pieces = 1 << ((nj * ni).bit_length() - 1)
        assert r % pieces == 0 and (r // pieces) % 16 == 0
        rs = r // pieces
        in_specs.append(pl.BlockSpec((None, rs, c), lambda j, i: (layer, jnp.minimum(j * ni + i, pieces - 1), 0)))
        out_specs.append(pl.BlockSpec((rs, c), lambda j, i: (jnp.minimum(j * ni + i, pieces - 1), 0)))
        out_shape.append(jax.ShapeDtypeStruct((r, c), BF16))
        args.append(side)
    outs = pl.pallas_call(
        functools.partial(_mm_w32_kernel, nw=nw, cw=cw, act=act, rp=rp, n_pieces=n_pieces,
                          has_side=side is not None),
        grid=(nj, ni), in_specs=in_specs, out_specs=out_specs, out_shape=out_shape,
        scratch_shapes=[pltpu.VMEM((2, k, tn), BF16)],
        compiler_params=_cparams(2, VMEM_LIMIT_MM), name=name,
    )(*args)
    return outs if side is not None else outs[0]


def _mm_acc_kernel(a_ref, w_ref, o_ref, acc_ref):
    kk = pl.program_id(2)

    @pl.when(kk == 0)
    def _():
        acc_ref[...] = jnp.zeros_like(acc_ref)

    acc_ref[...] += jnp.dot(a_ref[...], w_ref[...], preferred_element_type=F32)

    @pl.when(kk == pl.num_programs(2) - 1)
    def _():
        o_ref[...] = acc_ref[...].astype(o_ref.dtype)


def _mm_acc(a, w, layer, out_dtype, tm, tn, tk, name="mm_acc"):
    m, k = a.shape
    n = w.shape[2]
    tm, tn, tk = _tile(m, tm), _tile(n, tn), _tile(k, tk)
    return pl.pallas_call(
        _mm_acc_kernel,
        grid=(m // tm, n // tn, k // tk),
        in_specs=[pl.BlockSpec((tm, tk), lambda i, j, q: (i, q)),
                  pl.BlockSpec((None, tk, tn), lambda i, j, q: (layer, q, j))],
        out_specs=pl.BlockSpec((tm, tn), lambda i, j, q: (i, j)),
        out_shape=jax.ShapeDtypeStruct((m, n), out_dtype),
        scratch_shapes=[pltpu.VMEM((tm, tn), F32)],
        compiler_params=_cparams(3), name=name,
    )(a, w)


def _ln_kernel(*refs, alpha, has_res, n_out):
    x_ref = refs[0]
    r_ref = refs[1] if has_res else None
    g_ref, b_ref = refs[1 + has_res], refs[2 + has_res]
    outs = refs[3 + has_res:3 + has_res + n_out]
    x = x_ref[...].astype(F32)
    if has_res:
        x = alpha * x + r_ref[...].astype(F32)
    y = _layer_norm(x, g_ref[...], b_ref[...])
    for o_ref in outs:
        o_ref[...] = y.astype(o_ref.dtype)


def _ln_rows(x, res, g, b, alpha, tm, in_row0, nrows, out_dtypes, name="ln"):
    d = x.shape[1]
    tm = _tile(nrows, tm, 8)
    assert in_row0 % tm == 0
    ib = in_row0 // tm
    row_in = pl.BlockSpec((tm, d), lambda i: (ib + i, 0))
    row_out = pl.BlockSpec((tm, d), lambda i: (i, 0))
    vec = pl.BlockSpec((1, d), lambda i: (0, 0))
    has_res = res is not None
    args = [x] + ([res] if has_res else []) + [g.reshape(1, d), b.reshape(1, d)]
    specs = [row_in] * (1 + has_res) + [vec, vec]
    return pl.pallas_call(
        functools.partial(_ln_kernel, alpha=alpha, has_res=int(has_res), n_out=len(out_dtypes)),
        grid=(nrows // tm,), in_specs=specs, out_specs=[row_out] * len(out_dtypes),
        out_shape=[jax.ShapeDtypeStruct((nrows, d), dt) for dt in out_dtypes],
        compiler_params=_cparams(1), name=name,
    )(*args)


def _ln_in_kernel(xp_ref, xs_ref, g_ref, b_ref, ob_ref, *, n_first):
    x = jnp.where(pl.program_id(0) < n_first, xp_ref[...], xs_ref[...])
    ob_ref[...] = _layer_norm(x, g_ref[...], b_ref[...]).astype(BF16)


def _ln_in(xp, xs, g, b, tm):
    (n_p, d), n_s = xp.shape, xs.shape[0]
    tm = _tile(math.gcd(n_p, n_s), tm, 8)
    npb = n_p // tm
    row = pl.BlockSpec((tm, d), lambda i: (i, 0))
    vec = pl.BlockSpec((1, d), lambda i: (0, 0))
    return pl.pallas_call(
        functools.partial(_ln_in_kernel, n_first=npb),
        grid=((n_p + n_s) // tm,),
        in_specs=[pl.BlockSpec((tm, d), lambda i: (jnp.minimum(i, npb - 1), 0)),
                  pl.BlockSpec((tm, d), lambda i: (jnp.maximum(i - npb, 0), 0)), vec, vec],
        out_specs=row,
        out_shape=jax.ShapeDtypeStruct((n_p + n_s, d), BF16),
        compiler_params=_cparams(1), name="ln_in",
    )(xp, xs, g.reshape(1, d), b.reshape(1, d))


def _merge_kernel(*refs, n_first):
    yp_refs, ys_refs, g_refs, wb_ref, o_ref = refs[0:4], refs[4:8], refs[8:12], refs[12], refs[13]
    first = pl.program_id(1) < n_first
    acc = None
    for n in range(4):
        y = jnp.where(first, yp_refs[n][...], ys_refs[n][...])
        br = jnp.dot(y, wb_ref[n], preferred_element_type=F32)
        term = _sigmoid(g_refs[n][...].astype(F32)) * br
        acc = term if acc is None else acc + term
    o_ref[...] = acc.astype(o_ref.dtype)


def _merge(ys_p, ys_s, gates, wb, layer, tm, tn):
    (n_p, db), n_s = ys_p[0].shape, ys_s[0].shape[0]
    d = wb.shape[3]
    tm, tn = _tile(math.gcd(n_p, n_s), tm), _tile(d, tn)
    nj = d // tn
    npb = n_p // tm
    yp_spec = pl.BlockSpec((tm, db), lambda j, i: (jnp.minimum(i, npb - 1), 0))
    ys_spec = pl.BlockSpec((tm, db), lambda j, i: (jnp.maximum(i - npb, 0), 0))
    g_specs = [pl.BlockSpec((tm, tn), functools.partial(lambda j, i, n: (i, n * nj + j), n=n)) for n in range(4)]
    return pl.pallas_call(
        functools.partial(_merge_kernel, n_first=npb),
        grid=(nj, (n_p + n_s) // tm),
        in_specs=[yp_spec] * 4 + [ys_spec] * 4 + g_specs
                 + [pl.BlockSpec((None, 4, db, tn), lambda j, i: (layer, 0, 0, j))],
        out_specs=pl.BlockSpec((tm, tn), lambda j, i: (i, j)),
        out_shape=jax.ShapeDtypeStruct((n_p + n_s, d), BF16),
        compiler_params=_cparams(2), name="merge",
    )(*ys_p, *ys_s, gates, gates, gates, gates, wb)


def _mixer_a_kernel(a1_ref, a2_ref, hist_ref, cw_ref, cb_ref, g_ref, b_ref, y_ref, nh_ref, zext_ref, acc_ref,
                    *, tb, width, pad):
    hist = width - 1
    nchunk = zext_ref.shape[0]
    t = pl.program_id(1)

    @pl.when(t == 0)
    def _():
        h = hist_ref[0]
        for c in range(nchunk):
            zext_ref[c, 0:pad - hist, :] = jnp.zeros((pad - hist, LANES), F32)
            zext_ref[c, pad - hist:pad, :] = h[:, c * LANES:(c + 1) * LANES]

    z = a1_ref[...] * _sigmoid(a2_ref[...])
    for c in range(nchunk):
        zext_ref[c, pad:pad + tb, :] = z[:, c * LANES:(c + 1) * LANES]

    rc = 64

    def conv_chunk(c, carry):
        w = cw_ref[c]
        bias = cb_ref[c]
        for r0 in range(0, tb, rc):
            acc = jnp.broadcast_to(bias, (rc, LANES))
            for j in range(width):
                acc = acc + w[j:j + 1, :] * zext_ref[c, pl.ds(pad - hist + j + r0, rc), :]
            acc_ref[c, r0:r0 + rc, :] = acc
        return carry

    lax.fori_loop(0, nchunk, conv_chunk, 0)
    y = jnp.concatenate([acc_ref[c] for c in range(nchunk)], axis=1)
    y = _layer_norm(y, g_ref[...], b_ref[...])
    y_ref[...] = (y * _sigmoid(y)).astype(y_ref.dtype)
    nh_ref[0] = jnp.concatenate([zext_ref[c, tb + pad - hist:tb + pad, :] for c in range(nchunk)], axis=1)
    for c in range(nchunk):
        zext_ref[c, 0:pad, :] = zext_ref[c, tb:tb + pad, :]


def _mixer_c_kernel(xb_ref, gate_ref, hist_ref, h0_ref, cw_ref, cb_ref, wa_ref, ba_ref, wx_ref, bx_ref, lam_ref,
                    y_ref, nhist_ref, nh_ref, xext_ref, a_ref, u_ref, hs_ref, hcar_ref, *, tb, width, pad):
    hist = width - 1
    t = pl.program_id(1)
    db = xb_ref.shape[1]

    @pl.when(t == 0)
    def _():
        xext_ref[0:pad - hist, :] = jnp.zeros((pad - hist, db), F32)
        xext_ref[pad - hist:pad, :] = hist_ref[0]
        hcar_ref[...] = h0_ref[0]

    xext_ref[pad:pad + tb, :] = xb_ref[...]
    cw = cw_ref[...]
    xc = jnp.broadcast_to(cb_ref[...], (tb, db))
    for j in range(width):
        xc = xc + cw[j:j + 1, :] * xext_ref[pad - hist + j:pad - hist + j + tb, :]
    xcb = xc.astype(BF16)
    r = _sigmoid(jnp.dot(xcb, wa_ref[...], preferred_element_type=F32) + ba_ref[...])
    i = _sigmoid(jnp.dot(xcb, wx_ref[...], preferred_element_type=F32) + bx_ref[...])
    log_a = (LRU_C * r) * (-_softplus(-lam_ref[...]))
    a = jnp.exp(log_a)
    a_ref[...] = a
    u_ref[...] = jnp.sqrt(-jnp.tanh(log_a) * (a * a + 1.0)) * (i * xc)

    def step(k, h):
        h = a_ref[pl.ds(k, 1), :] * h + u_ref[pl.ds(k, 1), :]
        hs_ref[pl.ds(k, 1), :] = h
        return h

    h = lax.fori_loop(0, tb, step, hcar_ref[...], unroll=8)
    hcar_ref[...] = h
    y_ref[...] = (hs_ref[...] * _gelu(gate_ref[...])).astype(y_ref.dtype)
    nhist_ref[0] = xext_ref[tb + pad - hist:tb + pad, :]
    nh_ref[0] = h
    xext_ref[0:pad, :] = xext_ref[tb:tb + pad, :]


def _mixer_d_kernel(u_ref, v_ref, g_ref, b_ref, ws_ref, bias_ref, y_ref, *maybe_vrows_ref, tb, rows):
    v = _layer_norm(_gelu(v_ref[...]), g_ref[...], b_ref[...])
    for vrows_ref in maybe_vrows_ref:
        vrows_ref[...] = v
    vb = v.astype(BF16)
    npair = v.shape[1] // LANES
    first = _iota((rows, LANES), 1) < HEAD
    zero = jnp.zeros((rows, LANES), BF16)
    for r0 in range(0, tb, rows):
        for p in range(npair):
            ls = slice(p * LANES, (p + 1) * LANES)
            v2 = vb[r0:r0 + rows, ls]
            vbd = jnp.concatenate([jnp.where(first, v2, zero), jnp.where(first, zero, v2)], axis=0)
            s = jnp.dot(ws_ref[p], vbd, preferred_element_type=F32) + bias_ref[:, ls]
            y_ref[r0:r0 + rows, ls] = (_gelu(u_ref[r0:r0 + rows, ls]) * s).astype(y_ref.dtype)


def _segsum(x, ones_bd):
    outs = [jnp.dot(x[:, c * LANES:(c + 1) * LANES].astype(BF16), ones_bd, preferred_element_type=F32)
            for c in range(x.shape[1] // LANES)]
    return jnp.concatenate(outs, axis=1)


def _wkv_chunk(rows, at_c, rt_c, bt_c, kt_c, v_s, pc_c, st_ref, y_s, ap_ref, x_ref, ak_ref, ar_ref, rhs_ref, u_ref):
    c = WKV_CHUNK
    npair = st_ref.shape[0]
    lane = _iota((c, LANES), 1)
    first = lane < HEAD

    def bd(x):
        return jnp.concatenate([jnp.where(first, x, 0.0), jnp.where(first, 0.0, x)], axis=0)

    row = _iota((c, LANES), 0)
    col = jnp.where(first, lane, lane - HEAD)
    strict = row > col
    incl = row >= col
    eye = (_iota((LANES, LANES), 0) == _iota((LANES, LANES), 1)).astype(F32)
    same_head = (_iota((LANES, LANES), 0) < HEAD) == (_iota((LANES, LANES), 1) < HEAD)
    lanes = [slice(p * LANES, (p + 1) * LANES) for p in range(npair)]

    for p, ls in enumerate(lanes):
        g = _dot(jnp.concatenate([at_c[:, ls], rt_c[:, ls]], axis=0),
                 jnp.concatenate([bd(bt_c[:, ls]), bd(kt_c[:, ls])], axis=0), _NT)
        a_pow = bd(jnp.where(strict, g[:c, :LANES], 0.0))
        ap_ref[p] = a_pow.astype(BF16)
        x_ref[p] = eye + a_pow
        ak_ref[p] = jnp.where(strict, g[:c, LANES:], 0.0).astype(BF16)
        ar_ref[p] = jnp.concatenate([jnp.where(incl, g[c:, :LANES], 0.0),
                                     jnp.where(incl, g[c:, LANES:], 0.0)], axis=1).astype(BF16)

    for p in range(npair):
        a_pow = ap_ref[p]
        ap_ref[p] = jnp.dot(a_pow, a_pow, preferred_element_type=F32).astype(BF16)
    for step in range(5):
        for p in range(npair):
            x = x_ref[p]
            a_pow = ap_ref[p]
            if step < 4:
                both = jnp.dot(jnp.concatenate([x.astype(BF16), a_pow], axis=0), a_pow, preferred_element_type=F32)
                x_ref[p] = x + both[:LANES]
                ap_ref[p] = both[LANES:].astype(BF16)
            else:
                x_ref[p] = x + jnp.dot(x.astype(BF16), a_pow, preferred_element_type=F32)

    for p, ls in enumerate(lanes):
        rhs_ref[p] = _dot(at_c[:, ls], st_ref[p], _NT) + jnp.dot(ak_ref[p], bd(v_s[rows, ls]).astype(BF16),
                                                                  preferred_element_type=F32)
    for p in range(npair):
        x = x_ref[p]
        u_ref[p] = _dot(x[:c] + x[c:], bd(rhs_ref[p]))
    for p, ls in enumerate(lanes):
        u2 = u_ref[p]
        v2 = v_s[rows, ls]
        s0 = st_ref[p]
        y2 = _dot(rt_c[:, ls], s0, _NT) + jnp.dot(ar_ref[p], jnp.concatenate([bd(u2), bd(v2)], axis=0).astype(BF16),
                                                  preferred_element_type=F32)
        uvt = jnp.transpose(jnp.concatenate([u2, v2], axis=0))
        upd = _dot(uvt, jnp.concatenate([bt_c[:, ls], kt_c[:, ls]], axis=0))
        y_s[rows, ls] = y2
        st_ref[p] = jnp.where(same_head, (s0 + upd) * pc_c[:, ls], 0.0)


def _mixer_b_kernel(pr_ref, pk_ref, pv_ref, pl_ref, sh_ref, shl_ref, s0_ref,
                    mu_ref, mul_ref, w0_ref, ww2_ref, a0_ref, wa2_ref, wg2_ref, kk_ref, ka_ref, rk_ref, gg_ref, gb_ref,
                    y_ref, nsh_ref, nshl_ref, ns_ref,
                    pext_ref, lext_ref, st_ref, r_s, k_s, v_s, kk_s, b_s, ld_s, y_s,
                    at_c, rt_c, bt_c, kt_c, pc_c, ap_ref, x_ref, ak_ref, ar_ref, rhs_ref, u_ref, *, tb, pad):
    t = pl.program_id(1)
    db = pr_ref.shape[1]

    @pl.when(t == 0)
    def _():
        pext_ref[0:pad, :] = jnp.broadcast_to(sh_ref[0], (pad, 3 * db))
        lext_ref[0:pad, :] = jnp.broadcast_to(shl_ref[0], (pad, lext_ref.shape[1]))
        st_ref[...] = s0_ref[0]

    pext_ref[pad:pad + tb, 0:db] = pr_ref[...]
    pext_ref[pad:pad + tb, db:2 * db] = pk_ref[...]
    pext_ref[pad:pad + tb, 2 * db:3 * db] = pv_ref[...]
    lext_ref[pad:pad + tb, :] = pl_ref[...]
    p = pext_ref[pad:pad + tb, :]
    ps = p + mu_ref[...] * (pext_ref[pad - 1:pad - 1 + tb, :] - p)
    lo_ = lext_ref[pad:pad + tb, :]
    lo_s = lo_ + mul_ref[...] * (lext_ref[pad - 1:pad - 1 + tb, :] - lo_)
    nsh_ref[0] = pext_ref[pad + tb - 1:pad + tb, :]
    nshl_ref[0] = lext_ref[pad + tb - 1:pad + tb, :]
    pext_ref[0:pad, :] = pext_ref[tb:tb + pad, :]
    lext_ref[0:pad, :] = lext_ref[tb:tb + pad, :]

    r = ps[:, 0:db]
    k = ps[:, db:2 * db]
    v = ps[:, 2 * db:3 * db]
    wa_lo = lo_s[:, 0:LANES]
    g_lo = lo_s[:, LANES:2 * LANES]
    w = -_softplus(-(w0_ref[...] + jnp.dot(_lhs3(jnp.tanh(wa_lo)), ww2_ref[...], preferred_element_type=F32))) - 0.5
    a = _sigmoid(a0_ref[...] + jnp.dot(_lhs3(wa_lo), wa2_ref[...], preferred_element_type=F32))
    gate = jnp.dot(_lhs3(_sigmoid(g_lo)), wg2_ref[...], preferred_element_type=F32)

    ones2 = ((_iota((LANES, LANES), 0) < HEAD) == (_iota((LANES, LANES), 1) < HEAD)).astype(BF16)

    kk = k * kk_ref[...]
    kk = kk * lax.rsqrt(jnp.maximum(_segsum(kk * kk, ones2), 1e-24))
    k = k * (1.0 + (a - 1.0) * ka_ref[...])
    r_s[...] = r
    k_s[...] = k
    v_s[...] = v
    kk_s[...] = kk
    b_s[...] = kk * a
    ld_s[...] = -jnp.exp(w)

    c = WKV_CHUNK
    tri = (_iota((c, c), 0) >= _iota((c, c), 1)).astype(BF16)
    tri3 = jnp.concatenate([tri, tri, tri], axis=1)

    def chunk(ci, carry):
        rows = pl.ds(pl.multiple_of(ci * c, c), c)
        ld = ld_s[rows, :]
        hi = ld.astype(BF16)
        r1 = ld - hi.astype(F32)
        mid = r1.astype(BF16)
        lo = (r1 - mid.astype(F32)).astype(BF16)
        lc = jnp.dot(tri3, jnp.concatenate([hi, mid, lo], axis=0), preferred_element_type=F32)
        pcum = jnp.exp(lc)
        pinv = jnp.exp(-lc)
        at_c[...] = -(kk_s[rows, :] * jnp.exp(lc - ld))
        bt_c[...] = b_s[rows, :] * pinv
        kt_c[...] = k_s[rows, :] * pinv
        rt_c[...] = r_s[rows, :] * pcum
        pc_c[...] = pcum[c - 1:c, :]
        _wkv_chunk(rows, at_c, rt_c, bt_c, kt_c, v_s, pc_c, st_ref, y_s, ap_ref, x_ref, ak_ref, ar_ref, rhs_ref, u_ref)
        return carry

    lax.fori_loop(0, tb // c, chunk, 0)
    ns_ref[0] = st_ref[...]

    y = y_s[...]
    mu_y = _segsum(y, ones2) * (1.0 / HEAD)
    yc = y - mu_y
    var_y = _segsum(yc * yc, ones2) * (1.0 / HEAD)
    yn = yc * lax.rsqrt(var_y + GN_EPS_B) * gg_ref[...] + gb_ref[...]
    bonus = _segsum(r_s[...] * k_s[...] * rk_ref[...], ones2) * v_s[...]
    y_ref[...] = ((yn + bonus) * gate).astype(y_ref.dtype)


def _mixers(proj, lora, row0, n_streams, length, tb, st, prm, want_v, tag):
    db = prm["db"]
    nb = length // tb
    off = row0 // tb
    assert row0 % tb == 0 and length % tb == 0
    grid = (n_streams, nb)

    def rows(width, col):
        return pl.BlockSpec((tb, width), lambda s, t: (off + s * nb + t, col))

    def per_stream(shape):
        nd = len(shape)
        return pl.BlockSpec((1,) + shape, lambda s, t: (s,) + (0,) * nd)

    def whole(arr):
        nd = arr.ndim
        return pl.BlockSpec(arr.shape, lambda s, t: (0,) * nd)

    y_shape = jax.ShapeDtypeStruct((n_streams * length, db), BF16)
    y_spec = pl.BlockSpec((tb, db), lambda s, t: (s * nb + t, 0))
    nchunk = db // LANES

    def call(body, n, in_specs, args, out_specs, out_shape, scratch, name):
        return pl.pallas_call(
            body, grid=grid, in_specs=in_specs, out_specs=out_specs, out_shape=out_shape, scratch_shapes=scratch,
            compiler_params=_cparams(2), name=name + tag)(*args)

    wa = prm["conv_a_w"].shape[0]
    pad_a = 32
    cw_a = prm["conv_a_w"].reshape(wa, nchunk, LANES).transpose(1, 0, 2)
    cb_a = prm["conv_a_b"].reshape(nchunk, 1, LANES)
    y_a, n_a = call(
        functools.partial(_mixer_a_kernel, tb=tb, width=wa, pad=pad_a), 0,
        [rows(db, 0), rows(db, 1), per_stream((wa - 1, db)), whole(cw_a), whole(cb_a),
         whole(prm["ln_a_g"]), whole(prm["ln_a_b"])],
        [proj, proj, st["conv_a"], cw_a, cb_a, prm["ln_a_g"], prm["ln_a_b"]],
        [y_spec, per_stream((wa - 1, db))],
        [y_shape, jax.ShapeDtypeStruct((n_streams, wa - 1, db), F32)],
        [pltpu.VMEM((nchunk, tb + pad_a, LANES), F32), pltpu.VMEM((nchunk, tb, LANES), F32)], "mixer_a")

    pad_b = 8
    nl = lora.shape[1]
    c = WKV_CHUNK
    b_w = [prm[k] for k in ("mu_rkv", "mu_l", "w0", "ww2", "a0", "wa2", "wg2", "k_k", "k_a", "r_k", "gn_g", "gn_b")]
    y_b, n_sh, n_shl, n_wkv = call(
        functools.partial(_mixer_b_kernel, tb=tb, pad=pad_b), 1,
        [rows(db, 2), rows(db, 3), rows(db, 4), pl.BlockSpec((tb, nl), lambda s, t: (off + s * nb + t, 0)),
         per_stream((1, 3 * db)), per_stream((1, nl)), per_stream((nchunk, LANES, LANES))] + [whole(w) for w in b_w],
        [proj, proj, proj, lora, st["shift_rkv"], st["shift_l"], st["wkv"]] + b_w,
        [y_spec, per_stream((1, 3 * db)), per_stream((1, nl)), per_stream((nchunk, LANES, LANES))],
        [y_shape, jax.ShapeDtypeStruct((n_streams, 1, 3 * db), F32), jax.ShapeDtypeStruct((n_streams, 1, nl), F32),
         jax.ShapeDtypeStruct((n_streams, nchunk, LANES, LANES), F32)],
        [pltpu.VMEM((tb + pad_b, 3 * db), F32), pltpu.VMEM((tb + pad_b, nl), F32),
         pltpu.VMEM((nchunk, LANES, LANES), F32)] + [pltpu.VMEM((tb, db), F32)] * 7
        + [pltpu.VMEM((c, db), F32)] * 4 + [pltpu.VMEM((1, db), F32),
                                            pltpu.VMEM((nchunk, LANES, LANES), BF16),
                                            pltpu.VMEM((nchunk, LANES, LANES), F32),
                                            pltpu.VMEM((nchunk, c, LANES), BF16),
                                            pltpu.VMEM((nchunk, c, 2 * LANES), BF16),
                                            pltpu.VMEM((nchunk, c, LANES), F32),
                                            pltpu.VMEM((nchunk, c, LANES), F32)], "mixer_b")

    wc = prm["conv_c_w"].shape[0]
    pad_c = 8
    c_w = [prm[k] for k in ("conv_c_w", "conv_c_b", "lru_wa", "lru_ba", "lru_wx", "lru_bx", "lru_lambda")]
    y_c, n_cc, n_ch = call(
        functools.partial(_mixer_c_kernel, tb=tb, width=wc, pad=pad_c), 2,
        [rows(db, 5), rows(db, 6), per_stream((wc - 1, db)), per_stream((1, db))] + [whole(w) for w in c_w],
        [proj, proj, st["conv_c"], st["lru"]] + c_w,
        [y_spec, per_stream((wc - 1, db)), per_stream((1, db))],
        [y_shape, jax.ShapeDtypeStruct((n_streams, wc - 1, db), F32), jax.ShapeDtypeStruct((n_streams, 1, db), F32)],
        [pltpu.VMEM((tb + pad_c, db), F32)] + [pltpu.VMEM((tb, db), F32)] * 3 + [pltpu.VMEM((1, db), F32)], "mixer_c")

    rows_d = min(tb, MLP_CHUNK)
    ws = prm["ws_pair"] if rows_d == MLP_CHUNK else prm["ws_pair_half"]
    bias = prm["bias_d"][:rows_d]
    d_out_specs, d_out_shape = [y_spec], [y_shape]
    if want_v:
        d_out_specs.append(pl.BlockSpec((tb, db), lambda s, t: (s * nb + t, 0)))
        d_out_shape.append(jax.ShapeDtypeStruct((n_streams * length, db), F32))
    d_out = call(
        functools.partial(_mixer_d_kernel, tb=tb, rows=rows_d), 3,
        [rows(db, 7), rows(db, 8), whole(prm["ln_d_g"]), whole(prm["ln_d_b"]), whole(ws), whole(bias)],
        [proj, proj, prm["ln_d_g"], prm["ln_d_b"], ws, bias],
        d_out_specs, d_out_shape, [], "mixer_d")
    y_d = d_out[0]
    v_rows = d_out[1] if want_v else None

    new = dict(conv_a=n_a, shift_rkv=n_sh, shift_l=n_shl, wkv=n_wkv, conv_c=n_cc, lru=n_ch)
    return (y_a, y_b, y_c, y_d), new, v_rows


def _wkv_to_pairs(s):
    top = jnp.pad(s[:, 0::2], ((0, 0), (0, 0), (0, 0), (0, HEAD)))
    bot = jnp.pad(s[:, 1::2], ((0, 0), (0, 0), (0, 0), (HEAD, 0)))
    return jnp.concatenate([top, bot], axis=2)


def _wkv_from_pairs(z):
    n, hp = z.shape[0], z.shape[1]
    s = jnp.stack([z[:, :, :HEAD, :HEAD], z[:, :, HEAD:, HEAD:]], axis=2)
    return s.reshape(n, 2 * hp, HEAD, HEAD)


def _block_diag(w):
    n, b, _ = w.shape
    tiled = jnp.tile(w.reshape(n * b, b), (1, n))
    same = (jnp.arange(n * b)[:, None] // b) == (jnp.arange(n * b)[None, :] // b)
    return jnp.where(same, tiled, 0.0)


def _row(v):
    return v.reshape(1, -1)


def kernel(x_prompt, x_sample, state_conv_a, state_shift_b, state_wkv_b, state_conv_c, state_lru_c, ln_in_g, ln_in_b, w_in, conv_a_w, conv_a_b, ln_a_g, ln_a_b, mu_b, w0_b, w_w2_b, a0_b, w_a2_b, w_g2_b, k_k_b, k_a_b, r_k_b, gn_b_g, gn_b_b, conv_c_w, conv_c_b, lru_wa, lru_ba, lru_wx, lru_bx, lru_lambda, ln_d_g, ln_d_b, w_s_d, b_s_d, w_branch, w_out, ln1_g, ln1_b, w_up, w_down, ln2_g, ln2_b):
    batch, seq, d = x_prompt.shape
    dec_batch, dec_seq, _ = x_sample.shape
    depth = w_in.shape[0]
    db = d // 4
    d_ff = w_up.shape[2]
    n_heads = db // HEAD
    lw, la, lg = w_w2_b.shape[1], w_a2_b.shape[1], w_g2_b.shape[1]
    n_lora = lw + la + lg
    assert db % LANES == 0 and lw == HEAD and la == HEAD and lg == LANES and n_lora == WCOL
    assert w_s_d.shape[1] == db // HEAD and w_s_d.shape[2] == MLP_CHUNK
    assert dec_seq == STREAM_CHUNK
    tb_p = 256
    assert seq % tb_p == 0
    n_p, n_s = batch * seq, dec_batch * dec_seq
    n_all = n_p + n_s
    alpha = (2 * depth) ** 0.25
    blk_lora = (2 * db + 3 * db) // WCOL
    blk_gate = (9 * db + n_lora) // WCOL

    xb = _ln_in(x_prompt.reshape(n_p, d), x_sample.reshape(n_s, d), ln_in_g, ln_in_b, 256)

    zeros_p = dict(
        conv_a=jnp.zeros((batch, conv_a_w.shape[1] - 1, db), F32),
        shift_rkv=jnp.zeros((batch, 1, 3 * db), F32),
        shift_l=jnp.zeros((batch, 1, n_lora), F32),
        wkv=jnp.zeros((batch, n_heads // 2, LANES, LANES), F32),
        conv_c=jnp.zeros((batch, conv_c_w.shape[1] - 1, db), F32),
        lru=jnp.zeros((batch, 1, db), F32))

    blk = jnp.arange(MLP_CHUNK) // STREAM_CHUNK
    outs_p, outs_s, v_rows_s = [], [], []
    for l in range(depth):
        proj = _mm_w32(xb, w_in, l, 9 * db // WCOL, 0, blk_lora, WCOL, MM_TN, MM_TM, F32, name="proj_mix")
        lora = _mm_w32(xb, w_in, l, 1, blk_lora, None, WCOL, WCOL, 1024, F32, name="proj_lora")
        gates, w_branch_b = _mm_w32(xb, w_in, l, 4 * d // WCOL, blk_gate, None, WCOL, MM_TN, MM_TM, BF16,
                                    side=w_branch.reshape(depth, 4 * db, d), name="proj_gate")

        ws = jnp.where(blk[:, None] >= blk[None, :], w_s_d[l], 0.0).astype(BF16)
        half = STREAM_CHUNK
        zpad = jnp.zeros((HEAD, db), F32)
        prm = dict(
            db=db,
            conv_a_w=conv_a_w[l], conv_a_b=conv_a_b[l], ln_a_g=_row(ln_a_g[l]), ln_a_b=_row(ln_a_b[l]),
            mu_rkv=_row(mu_b[l][:3 * db]), mu_l=_row(mu_b[l][3 * db:]), w0=_row(w0_b[l]),
            ww2=_rhs3(jnp.concatenate([w_w2_b[l], zpad], axis=0)), a0=_row(a0_b[l]),
            wa2=_rhs3(jnp.concatenate([zpad, w_a2_b[l]], axis=0)), wg2=_rhs3(w_g2_b[l]),
            k_k=_row(k_k_b[l]), k_a=_row(k_a_b[l]), r_k=_row(r_k_b[l]), gn_g=_row(gn_b_g[l]), gn_b=_row(gn_b_b[l]),
            conv_c_w=conv_c_w[l], conv_c_b=_row(conv_c_b[l]),
            lru_wa=_block_diag(lru_wa[l]).astype(BF16), lru_ba=_row(lru_ba[l]),
            lru_wx=_block_diag(lru_wx[l]).astype(BF16), lru_bx=_row(lru_bx[l]),
            lru_lambda=_row(lru_lambda[l]),
            ln_d_g=_row(ln_d_g[l]), ln_d_b=_row(ln_d_b[l]),
            ws_pair=jnp.concatenate([ws[0::2], ws[1::2]], axis=2),
            ws_pair_half=jnp.concatenate([ws[0::2, :half, :half], ws[1::2, :half, :half]], axis=2),
            bias_d=jnp.repeat(b_s_d[l].T, HEAD, axis=1),
        )
        st_s = dict(
            conv_a=state_conv_a[l], shift_rkv=state_shift_b[l][:, None, :3 * db],
            shift_l=state_shift_b[l][:, None, 3 * db:], wkv=_wkv_to_pairs(state_wkv_b[l]),
            conv_c=state_conv_c[l], lru=state_lru_c[l][:, None, :])

        ys_p, new_p, _ = _mixers(proj, lora, 0, batch, seq, tb_p, zeros_p, prm, False, "_p")
        ys_s, new_s, v_rows = _mixers(proj, lora, n_p, dec_batch, dec_seq, dec_seq, st_s, prm, True, "_s")
        merged = _merge(ys_p, ys_s, gates, w_branch_b.reshape(1, 4, db, d), 0, 512, 1024)
        o = _mm_w32(merged, w_out, l, d // WCOL, 0, None, WCOL, MM_TN, MM_TM, BF16, name="out_proj")
        xb, = _ln_rows(xb, o, ln1_g[l], ln1_b[l], alpha, 256, 0, n_all, [BF16], name="ln1")
        hdn, w_down_b = _mm_w32(xb, w_up, l, d_ff // WCOL, 0, None, WCOL, MM_TN, MM_TM, BF16, act="relu2",
                                side=w_down, name="mlp_up")
        o = _mm_acc(hdn, w_down_b[None], 0, BF16, 1024, 1024, 4096, name="mlp_down")
        if l + 1 < depth:
            xb, = _ln_rows(xb, o, ln2_g[l], ln2_b[l], alpha, 256, 0, n_all, [BF16], name="ln2")
        else:
            y_prompt, = _ln_rows(xb, o, ln2_g[l], ln2_b[l], alpha, 256, 0, n_p, [F32], name="ln2_p")
            y_sample, = _ln_rows(xb, o, ln2_g[l], ln2_b[l], alpha, 256, n_p, n_s, [F32], name="ln2_s")
        outs_p.append(new_p)
        outs_s.append(new_s)
        v_rows_s.append(v_rows.reshape(dec_batch, dec_seq, db))

    def collect(outs):
        conv_a = jnp.stack([o["conv_a"] for o in outs])
        shift = jnp.stack([jnp.concatenate([o["shift_rkv"][:, 0], o["shift_l"][:, 0]], axis=1) for o in outs])
        wkv = jnp.stack([_wkv_from_pairs(o["wkv"]) for o in outs])
        conv_c = jnp.stack([o["conv_c"] for o in outs])
        lru = jnp.stack([o["lru"][:, 0] for o in outs])
        return conv_a, shift, wkv, conv_c, lru

    y_prompt = y_prompt.reshape(batch, seq, d)
    y_sample = y_sample.reshape(dec_batch, dec_seq, d)
    return (y_prompt, y_sample) + collect(outs_p) + collect(outs_s) + (jnp.stack(v_rows_s),)
```
